```python
import math
import jax, jax.numpy as jnp
from jax import lax
import numpy as np

D_MODEL = 1024
BATCH = 1
SEQ = 16384
DEPTH = 2
DEC_BATCH = 16
DEC_SEQ = 2048
PAST_LEN = 128

CONV_W = 512
CONV_K = 31
N_HEADS = 8
HEAD_DIM = 64
ATTN_W = N_HEADS * 2 * HEAD_DIM
Q_BLOCK = 128
ROPE_THETA = 10000.0
SGU_W = 512
SGU_GROUPS = 4
SGU_GROUP_W = SGU_W // SGU_GROUPS
CHUNK = 128
N_BRANCH = 3
RMS_EPS = 1e-6

_COLS = (
    2 * CONV_W,
    CONV_W,
    ATTN_W,
    ATTN_W,
    ATTN_W,
    ATTN_W,
    SGU_W,
    SGU_W,
    SGU_W,
    N_BRANCH * D_MODEL,
)
IN_COLS = sum(_COLS)
SPLITS = tuple(int(s) for s in np.cumsum(_COLS)[:-1])

kernel_name = "hybrid_conv_diffattn_gmlp_encoder"


def lambda_init_fn(layer_idx):
    return 0.8 - 0.6 * math.exp(-0.3 * layer_idx)


def rmsnorm(x, g):
    xf = x.astype(jnp.float32)
    y = xf * lax.rsqrt(jnp.mean(xf * xf, axis=-1, keepdims=True) + RMS_EPS)
    return (y * g.astype(jnp.float32)).astype(x.dtype)


def rotary(x, pos):
    half = HEAD_DIM // 2
    inv = ROPE_THETA ** (-jnp.arange(half, dtype=jnp.float32) / half)
    ang = pos.astype(jnp.float32)[:, None] * inv[None, :]
    cos = jnp.cos(ang)[None, :, None, None, :]
    sin = jnp.sin(ang)[None, :, None, None, :]
    xf = x.astype(jnp.float32)
    x1, x2 = xf[..., :half], xf[..., half:]
    out = jnp.concatenate([x1 * cos - x2 * sin, x2 * cos + x1 * sin], axis=-1)
    return out.astype(x.dtype)


def depthwise_conv(x, w, b):
    c = x.shape[-1]
    y = lax.conv_general_dilated(
        x, w[:, None, :].astype(x.dtype), window_strides=(1,),
        padding=[(CONV_K // 2, CONV_K // 2)],
        dimension_numbers=("NWC", "WIO", "NWC"), feature_group_count=c)
    return y + b.astype(x.dtype)


def diff_attention(q, k, v, lam):
    b, s = q.shape[0], q.shape[1]
    nblk = s // Q_BLOCK
    scale = 1.0 / math.sqrt(HEAD_DIM)
    qb = q.reshape(b, nblk, Q_BLOCK, N_HEADS, 2, HEAD_DIM).transpose(1, 0, 2, 3, 4, 5)
    kf = k.astype(jnp.float32)
    vf = v.astype(jnp.float32)

    def block(qblk):
        sc = jnp.einsum("bqhcd,bkhcd->bchqk", qblk.astype(jnp.float32), kf) * scale
        p = jax.nn.softmax(sc, axis=-1)
        a = p[:, 0] - lam * p[:, 1]
        return jnp.einsum("bhqk,bkhd->bqhd", a, vf)

    o = lax.map(block, qb)
    return o.transpose(1, 0, 2, 3, 4).reshape(b, s, N_HEADS, 2 * HEAD_DIM)


def mixer_layer(x, l, pos, norm_g, w_in, conv_w, conv_b, conv_norm_g, w_proj_a,
                lam_q1, lam_k1, lam_q2, lam_k2, subln_g, w_proj_b,
                sgu_norm_g, sgu_w, sgu_b, w_proj_c, w_out):
    b, s, _ = x.shape
    h = rmsnorm(x, norm_g)
    z = h @ w_in
    a_in, a_gate, q, k, v, b_gate, c_u, c_v, c_gate, merge = jnp.split(z, SPLITS, axis=-1)

    a = a_in[..., :CONV_W] * jax.nn.sigmoid(a_in[..., CONV_W:])
    a = depthwise_conv(a, conv_w, conv_b)
    a = jax.nn.silu(rmsnorm(a, conv_norm_g))
    ya = (a * jax.nn.silu(a_gate)) @ w_proj_a

    lam_init = lambda_init_fn(l)
    q = rotary(q.reshape(b, s, N_HEADS, 2, HEAD_DIM), pos)
    k = rotary(k.reshape(b, s, N_HEADS, 2, HEAD_DIM), pos)
    v = v.reshape(b, s, N_HEADS, 2 * HEAD_DIM)
    lam = (jnp.exp(jnp.sum(lam_q1.astype(jnp.float32) * lam_k1.astype(jnp.float32)))
           - jnp.exp(jnp.sum(lam_q2.astype(jnp.float32) * lam_k2.astype(jnp.float32)))
           + lam_init)
    o = diff_attention(q, k, v, lam)
    o = rmsnorm(o, subln_g) * (1.0 - lam_init)
    yb = (o.reshape(b, s, ATTN_W).astype(x.dtype) * jax.nn.silu(b_gate)) @ w_proj_b

    cv = rmsnorm(c_v, sgu_norm_g).reshape(b, s // CHUNK, CHUNK, SGU_GROUPS, SGU_GROUP_W)
    mixed = jnp.einsum("gpq,bnqgd->bnpgd", sgu_w, cv) + sgu_b.T[None, None, :, :, None]
    yc = (c_u * mixed.reshape(b, s, SGU_W) * jax.nn.silu(c_gate)) @ w_proj_c

    m = jax.nn.sigmoid(merge).reshape(b, s, N_BRANCH, D_MODEL)
    y = m[:, :, 0] * ya + m[:, :, 1] * yb + m[:, :, 2] * yc
    return x + y @ w_out


def trunk(x, norm_g, w_in, conv_w, conv_b, conv_norm_g, w_proj_a,
          lam_q1, lam_k1, lam_q2, lam_k2, subln_g, w_proj_b,
          sgu_norm_g, sgu_w, sgu_b, w_proj_c, w_out, final_g):
    pos = jnp.arange(x.shape[1], dtype=jnp.int32)
    for l in range(DEPTH):
        x = mixer_layer(x, l, pos, norm_g[l], w_in[l], conv_w[l], conv_b[l], conv_norm_g[l], w_proj_a[l],
                        lam_q1[l], lam_k1[l], lam_q2[l], lam_k2[l], subln_g[l], w_proj_b[l],
                        sgu_norm_g[l], sgu_w[l], sgu_b[l], w_proj_c[l], w_out[l])
    return rmsnorm(x, final_g)


def setup_inputs(seed: int = 0) -> dict:
    key = jax.random.key(seed)
    ks = jax.random.split(key, 22)
    f32 = jnp.float32

    def nrm(k, shape, scale):
        return jax.random.normal(k, shape, f32) * scale

    return {
        "x_prompt": nrm(ks[0], (BATCH, SEQ, D_MODEL), 1.0),
        "x_sample": nrm(ks[1], (DEC_BATCH, DEC_SEQ, D_MODEL), 1.0),
        "norm_g": 1.0 + nrm(ks[2], (DEPTH, D_MODEL), 0.02),
        "w_in": nrm(ks[3], (DEPTH, D_MODEL, IN_COLS), D_MODEL ** -0.5),
        "conv_w": nrm(ks[4], (DEPTH, CONV_K, CONV_W), CONV_K ** -0.5),
        "conv_b": nrm(ks[5], (DEPTH, CONV_W), 0.02),
        "conv_norm_g": 1.0 + nrm(ks[6], (DEPTH, CONV_W), 0.02),
        "w_proj_a": nrm(ks[7], (DEPTH, CONV_W, D_MODEL), CONV_W ** -0.5),
        "lam_q1": nrm(ks[8], (DEPTH, HEAD_DIM), 0.1),
        "lam_k1": nrm(ks[9], (DEPTH, HEAD_DIM), 0.1),
        "lam_q2": nrm(ks[10], (DEPTH, HEAD_DIM), 0.1),
        "lam_k2": nrm(ks[11], (DEPTH, HEAD_DIM), 0.1),
        "subln_g": 1.0 + nrm(ks[12], (DEPTH, 2 * HEAD_DIM), 0.02),
        "w_proj_b": nrm(ks[13], (DEPTH, ATTN_W, D_MODEL), ATTN_W ** -0.5),
        "sgu_norm_g": 1.0 + nrm(ks[14], (DEPTH, SGU_W), 0.02),
        "sgu_w": nrm(ks[15], (DEPTH, SGU_GROUPS, CHUNK, CHUNK), CHUNK ** -0.5),
        "sgu_b": 1.0 + nrm(ks[16], (DEPTH, SGU_GROUPS, CHUNK), 0.02),
        "w_proj_c": nrm(ks[17], (DEPTH, SGU_W, D_MODEL), SGU_W ** -0.5),
        "w_out": nrm(ks[18], (DEPTH, D_MODEL, D_MODEL), D_MODEL ** -0.5),
        "final_g": 1.0 + nrm(ks[19], (D_MODEL,), 0.02),
    }


def reference(x_prompt, x_sample, norm_g, w_in, conv_w, conv_b, conv_norm_g, w_proj_a,
              lam_q1, lam_k1, lam_q2, lam_k2, subln_g, w_proj_b,
              sgu_norm_g, sgu_w, sgu_b, w_proj_c, w_out, final_g):
    y_prompt = trunk(x_prompt, norm_g, w_in, conv_w, conv_b, conv_norm_g, w_proj_a,
                     lam_q1, lam_k1, lam_q2, lam_k2, subln_g, w_proj_b,
                     sgu_norm_g, sgu_w, sgu_b, w_proj_c, w_out, final_g)
    y_sample = trunk(x_sample, norm_g, w_in, conv_w, conv_b, conv_norm_g, w_proj_a,
                     lam_q1, lam_k1, lam_q2, lam_k2, subln_g, w_proj_b,
                     sgu_norm_g, sgu_w, sgu_b, w_proj_c, w_out, final_g)
    return (y_prompt, y_sample)
```

```python
import functools
import math

import jax
import jax.numpy as jnp
from jax import lax
from jax.experimental import pallas as pl
from jax.experimental.pallas import tpu as pltpu

D_MODEL = 1024
DEPTH = 2
CONV_W = 512
CONV_K = 31
CONV_HALO = 16
SUBLANES = 8
N_HEADS = 8
HEAD_DIM = 64
HEAD_W = 2 * HEAD_DIM
ATTN_W = N_HEADS * HEAD_W
ROPE_THETA = 10000.0
SGU_W = 512
SGU_GROUPS = 4
SGU_GROUP_W = SGU_W // SGU_GROUPS
CHUNK = 128
N_BRANCH = 3
RMS_EPS = 1e-6

_COLS = (2 * CONV_W, CONV_W, ATTN_W, ATTN_W, ATTN_W, ATTN_W, SGU_W, SGU_W, SGU_W, N_BRANCH * D_MODEL)
_OFF = [0]
for _c in _COLS:
    _OFF.append(_OFF[-1] + _c)
(O_AIN, O_AGATE, O_Q, O_K, O_V, O_BGATE, O_CU, O_CV, O_CGATE, O_MERGE, IN_COLS) = _OFF

F32 = jnp.float32
BF16 = jnp.bfloat16

VMEM_LIMIT_BYTES = 56 * 1024 * 1024


def _lambda_init(layer_idx):
    return 0.8 - 0.6 * math.exp(-0.3 * layer_idx)


def _rms(x, g):
    return x * lax.rsqrt(jnp.mean(x * x, axis=-1, keepdims=True) + RMS_EPS) * g


def _dot(a, b):
    return jnp.dot(a, b, preferred_element_type=F32)


def _tiles(seq):
    tok = 256
    tq = min(seq, 1024)
    tqi = 128
    tk = min(seq, 512)
    return tok, tq, tqi, tk


def _pre_kernel(x_ref, g_ref, w_ref, cs_ref, sgug_ref, sguw_ref, sgub_ref, wc_ref,
                aglu_ref, ga_ref, q_ref, kt_ref, v_ref, gb_ref, pc_ref, m01_ref,
                mix_ref):
    tok = x_ref.shape[1]
    h = _rms(x_ref[0], g_ref[...]).astype(BF16)

    def proj(c0, c1):
        return _dot(h, w_ref[:, c0:c1])

    za = proj(O_AIN, O_AGATE)
    aglu_ref[0] = (za[:, :CONV_W] * jax.nn.sigmoid(za[:, CONV_W:])).astype(BF16)
    ga_ref[0] = jax.nn.silu(proj(O_AGATE, O_Q)).astype(BF16)

    cos = cs_ref[:, :HEAD_W]
    sin = cs_ref[:, HEAD_W:]
    scale = 1.0 / math.sqrt(HEAD_DIM)
    zq = proj(O_Q, O_K)
    for hd in range(N_HEADS):
        s = zq[:, hd * HEAD_W:(hd + 1) * HEAD_W]
        q_ref[0, hd] = ((s * cos + pltpu.roll(s, HEAD_DIM, 1) * sin) * scale).astype(BF16)
    zk = proj(O_K, O_V)
    for hd in range(N_HEADS):
        s = zk[:, hd * HEAD_W:(hd + 1) * HEAD_W]
        kt_ref[0, hd] = (s * cos + pltpu.roll(s, HEAD_DIM, 1) * sin).T.astype(BF16)
    zv = proj(O_V, O_BGATE)
    for hd in range(N_HEADS):
        v_ref[0, hd] = zv[:, hd * HEAD_W:(hd + 1) * HEAD_W].astype(BF16)
    gb_ref[0] = jax.nn.silu(proj(O_BGATE, O_CU)).astype(BF16)

    cv = _rms(proj(O_CV, O_CGATE), sgug_ref[...]).astype(BF16)
    for n in range(tok // CHUNK):
        for g in range(SGU_GROUPS):
            blk = cv[n * CHUNK:(n + 1) * CHUNK, g * SGU_GROUP_W:(g + 1) * SGU_GROUP_W]
            mix_ref[n * CHUNK:(n + 1) * CHUNK, g * SGU_GROUP_W:(g + 1) * SGU_GROUP_W] = (
                _dot(sguw_ref[g], blk) + sgub_ref[:, g * SGU_GROUP_W:(g + 1) * SGU_GROUP_W])
    u = proj(O_CU, O_CV) * mix_ref[...] * jax.nn.silu(proj(O_CGATE, O_MERGE))
    yc = _dot(u.astype(BF16), wc_ref[...])

    m01_ref[0] = jax.nn.sigmoid(proj(O_MERGE, O_MERGE + 2 * D_MODEL)).astype(BF16)
    pc_ref[0] = (jax.nn.sigmoid(proj(O_MERGE + 2 * D_MODEL, IN_COLS)) * yc).astype(BF16)


def _pre_call(x, g, w_in, cs, sgug, sguw, sgub, wc):
    b, seq, _ = x.shape
    tok = _tiles(seq)[0]
    grid = (b, seq // tok)
    const2 = lambda bi, i: (0, 0)
    const3 = lambda bi, i: (0, 0, 0)
    tile = lambda bi, i: (bi, i, 0)
    head_tile = lambda bi, i: (bi, 0, i, 0)
    single = pl.Buffered(1)
    in_specs = [
        pl.BlockSpec((1, tok, D_MODEL), tile),
        pl.BlockSpec((1, D_MODEL), const2),
        pl.BlockSpec((D_MODEL, IN_COLS), const2, pipeline_mode=single),
        pl.BlockSpec((tok, 2 * HEAD_W), lambda bi, i: (i, 0)),
        pl.BlockSpec((1, SGU_W), const2),
        pl.BlockSpec((SGU_GROUPS, CHUNK, CHUNK), const3),
        pl.BlockSpec((CHUNK, SGU_W), const2),
        pl.BlockSpec((SGU_W, D_MODEL), const2),
    ]
    out_shape = [
        jax.ShapeDtypeStruct((b, seq, CONV_W), BF16),
        jax.ShapeDtypeStruct((b, seq, CONV_W), BF16),
        jax.ShapeDtypeStruct((b, N_HEADS, seq, HEAD_W), BF16),
        jax.ShapeDtypeStruct((b, N_HEADS, HEAD_W, seq), BF16),
        jax.ShapeDtypeStruct((b, N_HEADS, seq, HEAD_W), BF16),
        jax.ShapeDtypeStruct((b, seq, ATTN_W), BF16),
        jax.ShapeDtypeStruct((b, seq, D_MODEL), BF16),
        jax.ShapeDtypeStruct((b, seq, 2 * D_MODEL), BF16),
    ]
    out_specs = [
        pl.BlockSpec((1, tok, CONV_W), tile),
        pl.BlockSpec((1, tok, CONV_W), tile),
        pl.BlockSpec((1, N_HEADS, tok, HEAD_W), head_tile),
        pl.BlockSpec((1, N_HEADS, HEAD_W, tok), lambda bi, i: (bi, 0, 0, i)),
        pl.BlockSpec((1, N_HEADS, tok, HEAD_W), head_tile),
        pl.BlockSpec((1, tok, ATTN_W), tile),
        pl.BlockSpec((1, tok, D_MODEL), tile),
        pl.BlockSpec((1, tok, 2 * D_MODEL), tile),
    ]
    return pl.pallas_call(
        _pre_kernel,
        grid=grid,
        in_specs=in_specs,
        out_specs=out_specs,
        out_shape=out_shape,
        scratch_shapes=[pltpu.VMEM((tok, SGU_W), F32)],
        compiler_params=pltpu.CompilerParams(
            dimension_semantics=("arbitrary", "arbitrary"),
            vmem_limit_bytes=VMEM_LIMIT_BYTES),
        name="pre",
    )(x, g, w_in, cs, sgug, sguw, sgub, wc)


def _attn_kernel(lam_ref, g_ref, q_ref, kt_ref, v_ref, o_ref, *, lam_init, tqi, tk):
    tq = q_ref.shape[2]
    seq = kt_ref.shape[3]
    lam = (jnp.exp(jnp.sum(lam_ref[0:1, :] * lam_ref[1:2, :], axis=-1, keepdims=True))
           - jnp.exp(jnp.sum(lam_ref[2:3, :] * lam_ref[3:4, :], axis=-1, keepdims=True))
           + lam_init)
    lane = lax.broadcasted_iota(jnp.int32, (1, HEAD_W), 1)
    is_map0 = (lane % HEAD_DIM) < (HEAD_DIM // 2)
    zero = jnp.zeros((), BF16)

    def q_block(qi, carry):
        q0 = pl.multiple_of(qi * tqi, tqi)
        q = q_ref[0, 0, pl.ds(q0, tqi), :]
        q2 = jnp.concatenate([jnp.where(is_map0, q, zero), jnp.where(is_map0, zero, q)], axis=0)

        def k_step(kc, mla):
            m, l, acc = mla
            k0 = pl.multiple_of(kc * tk, tk)
            s = _dot(q2, kt_ref[0, 0, :, pl.ds(k0, tk)])
            m_new = jnp.maximum(m, jnp.max(s, axis=-1, keepdims=True))
            alpha = jnp.exp(m - m_new)
            p = jnp.exp(s - m_new)
            l = alpha * l + jnp.sum(p, axis=-1, keepdims=True)
            acc = alpha * acc + _dot(p.astype(BF16), v_ref[0, 0, pl.ds(k0, tk), :])
            return m_new, l, acc

        init = (jnp.full((2 * tqi, 1), -jnp.inf, F32), jnp.zeros((2 * tqi, 1), F32),
                jnp.zeros((2 * tqi, HEAD_W), F32))
        _, l, acc = lax.fori_loop(0, seq // tk, k_step, init)
        o = acc[:tqi] / l[:tqi] - lam * (acc[tqi:] / l[tqi:])
        o = _rms(o, g_ref[...]) * (1.0 - lam_init)
        o_ref[0, pl.ds(q0, tqi), :] = o.astype(o_ref.dtype)
        return carry

    lax.fori_loop(0, tq // tqi, q_block, 0)


def _attn_call(lam_vecs, subln_g, q, kt, v, lam_init):
    b, _, seq, _ = q.shape
    _, tq, tqi, tk = _tiles(seq)
    grid = (b, N_HEADS, seq // tq)
    const2 = lambda bi, hd, i: (0, 0)
    return pl.pallas_call(
        functools.partial(_attn_kernel, lam_init=lam_init, tqi=tqi, tk=tk),
        grid=grid,
        in_specs=[
            pl.BlockSpec((4, HEAD_DIM), const2),
            pl.BlockSpec((1, HEAD_W), const2),
            pl.BlockSpec((1, 1, tq, HEAD_W), lambda bi, hd, i: (bi, hd, i, 0)),
            pl.BlockSpec((1, 1, HEAD_W, seq), lambda bi, hd, i: (bi, hd, 0, 0)),
            pl.BlockSpec((1, 1, seq, HEAD_W), lambda bi, hd, i: (bi, hd, 0, 0)),
        ],
        out_specs=pl.BlockSpec((1, tq, HEAD_W), lambda bi, hd, i: (bi, i, hd)),
        out_shape=jax.ShapeDtypeStruct((b, seq, ATTN_W), BF16),
        compiler_params=pltpu.CompilerParams(
            dimension_semantics=("arbitrary", "arbitrary", "arbitrary"),
            vmem_limit_bytes=VMEM_LIMIT_BYTES),
        name="attn",
    )(lam_vecs, subln_g, q, kt, v)


def _post_kernel(x_ref, aprev_ref, acur_ref, anext_ref, ga_ref, o_ref, gb_ref, pc_ref, m01_ref,
                 cw_ref, cb_ref, cg_ref, wa_ref, wb_ref, wo_ref, fg_ref,
                 y_ref, abuf_ref, conv_ref, *, final, rows):
    tok = x_ref.shape[1]
    i = pl.program_id(1)
    last = pl.num_programs(1) - 1
    prev = aprev_ref[0].astype(F32)
    nxt = anext_ref[0].astype(F32)
    abuf_ref[0:CONV_HALO, :] = jnp.where(i > 0, prev, 0.0)
    abuf_ref[CONV_HALO:CONV_HALO + tok, :] = acur_ref[0].astype(F32)
    abuf_ref[CONV_HALO + tok:, :] = jnp.where(i < last, nxt, 0.0)

    shift = CONV_HALO - CONV_K // 2
    win_rows = rows + 2 * CONV_HALO

    def conv_rows(r, carry):
        r0 = pl.multiple_of(r * rows, rows)
        win = abuf_ref[pl.ds(r0, win_rows), :]
        acc = jnp.zeros((rows, CONV_W), F32)
        for s in range(SUBLANES):
            ws = win if s == 0 else pltpu.roll(win, win_rows - s, 0)
            for a in range(win_rows // SUBLANES):
                k = a * SUBLANES + s - shift
                if 0 <= k < CONV_K:
                    acc = acc + cw_ref[k:k + 1, :] * ws[a * SUBLANES:a * SUBLANES + rows]
        conv_ref[pl.ds(r0, rows), :] = acc
        return carry

    lax.fori_loop(0, tok // rows, conv_rows, 0)

    a = jax.nn.silu(_rms(conv_ref[...] + cb_ref[...], cg_ref[...]))
    ya = _dot((a * ga_ref[0].astype(F32)).astype(BF16), wa_ref[...])
    yb = _dot((o_ref[0].astype(F32) * gb_ref[0].astype(F32)).astype(BF16), wb_ref[...])
    y = (m01_ref[0, :, :D_MODEL].astype(F32) * ya + m01_ref[0, :, D_MODEL:].astype(F32) * yb
         + pc_ref[0].astype(F32))
    out = x_ref[0] + _dot(y.astype(BF16), wo_ref[...])
    if final:
        out = _rms(out, fg_ref[...])
    y_ref[0] = out


def _post_call(x, aglu, ga, o, gb, pc, m01, cw, cb, cg, wa, wb, wo, fg, final):
    b, seq, _ = x.shape
    tok = _tiles(seq)[0]
    halo_blocks = tok // CONV_HALO
    n_halo = seq // CONV_HALO
    grid = (b, seq // tok)
    tile = lambda bi, i: (bi, i, 0)
    const2 = lambda bi, i: (0, 0)
    in_specs = [
        pl.BlockSpec((1, tok, D_MODEL), tile),
        pl.BlockSpec((1, CONV_HALO, CONV_W),
                     lambda bi, i: (bi, jnp.maximum(i * halo_blocks - 1, 0), 0)),
        pl.BlockSpec((1, tok, CONV_W), tile),
        pl.BlockSpec((1, CONV_HALO, CONV_W),
                     lambda bi, i: (bi, jnp.minimum((i + 1) * halo_blocks, n_halo - 1), 0)),
        pl.BlockSpec((1, tok, CONV_W), tile),
        pl.BlockSpec((1, tok, ATTN_W), tile),
        pl.BlockSpec((1, tok, ATTN_W), tile),
        pl.BlockSpec((1, tok, D_MODEL), tile),
        pl.BlockSpec((1, tok, 2 * D_MODEL), tile),
        pl.BlockSpec((CONV_K, CONV_W), const2),
        pl.BlockSpec((1, CONV_W), const2),
        pl.BlockSpec((1, CONV_W), const2),
        pl.BlockSpec((CONV_W, D_MODEL), const2),
        pl.BlockSpec((ATTN_W, D_MODEL), const2),
        pl.BlockSpec((D_MODEL, D_MODEL), const2),
        pl.BlockSpec((1, D_MODEL), const2),
    ]
    return pl.pallas_call(
        functools.partial(_post_kernel, final=final, rows=32),
        grid=grid,
        in_specs=in_specs,
        out_specs=pl.BlockSpec((1, tok, D_MODEL), tile),
        out_shape=jax.ShapeDtypeStruct((b, seq, D_MODEL), F32),
        scratch_shapes=[pltpu.VMEM((tok + 2 * CONV_HALO, CONV_W), F32),
                        pltpu.VMEM((tok, CONV_W), F32)],
        compiler_params=pltpu.CompilerParams(
            dimension_semantics=("arbitrary", "arbitrary"),
            vmem_limit_bytes=VMEM_LIMIT_BYTES),
        name="post",
    )(x, aglu, aglu, aglu, ga, o, gb, pc, m01, cw, cb, cg, wa, wb, wo, fg)


def _permute_heads(w):
    k = w.shape[0]
    w = w.reshape(k, N_HEADS, 2, 2, HEAD_DIM // 2)
    return w.transpose(0, 1, 3, 2, 4).reshape(k, ATTN_W)


def _rotary_table(seq):
    half = HEAD_DIM // 2
    inv = ROPE_THETA ** (-jnp.arange(half, dtype=F32) / half)
    ang = jnp.arange(seq, dtype=jnp.int32).astype(F32)[:, None] * inv[None, :]
    cos = jnp.tile(jnp.cos(ang), (1, 4))
    sin = jnp.tile(jnp.sin(ang), (1, 4))
    sign = jnp.where(jnp.arange(HEAD_W) < HEAD_DIM, -1.0, 1.0).astype(F32)
    return jnp.concatenate([cos, sin * sign[None, :]], axis=1)


def kernel(x_prompt, x_sample, norm_g, w_in, conv_w, conv_b, conv_norm_g, w_proj_a,
           lam_q1, lam_k1, lam_q2, lam_k2, subln_g, w_proj_b,
           sgu_norm_g, sgu_w, sgu_b, w_proj_c, w_out, final_g):
    w_in_p = jnp.concatenate(
        [w_in[:, :, :O_Q], jax.vmap(_permute_heads)(w_in[:, :, O_Q:O_K]),
         jax.vmap(_permute_heads)(w_in[:, :, O_K:O_V]), w_in[:, :, O_V:]], axis=-1).astype(BF16)
    sgu_w_b = sgu_w.astype(BF16)
    sgu_bias = jnp.repeat(jnp.swapaxes(sgu_b, 1, 2), SGU_GROUP_W, axis=2)
    wc_b = w_proj_c.astype(BF16)
    wa_b = w_proj_a.astype(BF16)
    wb_b = w_proj_b.astype(BF16)
    wo_b = w_out.astype(BF16)
    lam_vecs = jnp.stack([lam_q1, lam_k1, lam_q2, lam_k2], axis=1).astype(F32)
    cs = _rotary_table(max(x_prompt.shape[1], x_sample.shape[1]))

    outs = []
    for x in (x_prompt, x_sample):
        for l in range(DEPTH):
            aglu, ga, q, kt, v, gb, pc, m01 = _pre_call(
                x, norm_g[l][None], w_in_p[l], cs, sgu_norm_g[l][None], sgu_w_b[l], sgu_bias[l],
                wc_b[l])
            o = _attn_call(lam_vecs[l], subln_g[l][None], q, kt, v, _lambda_init(l))
            x = _post_call(x, aglu, ga, o, gb, pc, m01, conv_w[l], conv_b[l][None],
                           conv_norm_g[l][None], wa_b[l], wb_b[l], wo_b[l], final_g[None],
                           final=(l == DEPTH - 1))
        outs.append(x)
    return tuple(outs)
```

```python
import functools
import math

import jax
import jax.numpy as jnp
from jax import lax
from jax.experimental import pallas as pl
from jax.experimental.pallas import tpu as pltpu

D_MODEL = 1024
DEPTH = 2
CONV_W = 512
CONV_K = 31
CONV_HALO = 16
SUBLANES = 8
N_HEADS = 8
HEAD_DIM = 64
HEAD_W = 2 * HEAD_DIM
ATTN_W = N_HEADS * HEAD_W
ROPE_THETA = 10000.0
SGU_W = 512
SGU_GROUPS = 4
SGU_GROUP_W = SGU_W // SGU_GROUPS
CHUNK = 128
N_BRANCH = 3
RMS_EPS = 1e-6
KEY_BLOCK = 128
PIPE_DIST = 1

_COLS = (2 * CONV_W, CONV_W, ATTN_W, ATTN_W, ATTN_W, ATTN_W, SGU_W, SGU_W, SGU_W, N_BRANCH * D_MODEL)
_OFF = [0]
for _c in _COLS:
    _OFF.append(_OFF[-1] + _c)
(O_AIN, O_AGATE, O_Q, O_K, O_V, O_BGATE, O_CU, O_CV, O_CGATE, O_MERGE, IN_COLS) = _OFF

F32 = jnp.float32
BF16 = jnp.bfloat16

VMEM_LIMIT_BYTES = 56 * 1024 * 1024


def _lambda_init(layer_idx):
    return 0.8 - 0.6 * math.exp(-0.3 * layer_idx)


def _rms(x, g):
    return x * lax.rsqrt(jnp.mean(x * x, axis=-1, keepdims=True) + RMS_EPS) * g


def _dot(a, b):
    return jnp.dot(a, b, preferred_element_type=F32)


def _tiles(seq):
    tok = 256
    tq = min(seq, 2048) if seq <= 2048 else 1024
    tqi = 128
    tk = 1024
    return tok, tq, tqi, tk


def _pre_kernel(x_ref, g_ref, w_ref, cs_ref, sgug_ref, sguw_ref, sgub_ref, wc_ref,
                aglu_ref, ga_ref, qt_ref, k_ref, vt_ref, gb_ref, pc_ref, m01_ref,
                mix_ref):
    tok = x_ref.shape[1]
    h = _rms(x_ref[0], g_ref[...]).astype(BF16)

    def proj(c0, c1):
        return _dot(h, w_ref[:, c0:c1])

    za = proj(O_AIN, O_AGATE)
    aglu_ref[0] = (za[:, :CONV_W] * jax.nn.sigmoid(za[:, CONV_W:])).astype(BF16)
    ga_ref[0] = jax.nn.silu(proj(O_AGATE, O_Q)).astype(BF16)

    cos = cs_ref[:, :HEAD_W]
    sin = cs_ref[:, HEAD_W:]
    scale = math.log2(math.e) / math.sqrt(HEAD_DIM)
    zq = proj(O_Q, O_K)
    for hd in range(N_HEADS):
        s = zq[:, hd * HEAD_W:(hd + 1) * HEAD_W]
        qt_ref[0, hd] = ((s * cos + pltpu.roll(s, HEAD_DIM, 1) * sin) * scale).T.astype(BF16)
    zk = proj(O_K, O_V)
    for hd in range(N_HEADS):
        s = zk[:, hd * HEAD_W:(hd + 1) * HEAD_W]
        k_ref[0, hd] = (s * cos + pltpu.roll(s, HEAD_DIM, 1) * sin).astype(BF16)
    zv = proj(O_V, O_BGATE)
    for hd in range(N_HEADS):
        for c in range(tok // KEY_BLOCK):
            vt_ref[0, hd, c] = zv[c * KEY_BLOCK:(c + 1) * KEY_BLOCK,
                                  hd * HEAD_W:(hd + 1) * HEAD_W].T.astype(BF16)
    gb_ref[0] = jax.nn.silu(proj(O_BGATE, O_CU)).astype(BF16)

    cv = _rms(proj(O_CV, O_CGATE), sgug_ref[...]).astype(BF16)
    for n in range(tok // CHUNK):
        for g in range(SGU_GROUPS):
            blk = cv[n * CHUNK:(n + 1) * CHUNK, g * SGU_GROUP_W:(g + 1) * SGU_GROUP_W]
            mix_ref[n * CHUNK:(n + 1) * CHUNK, g * SGU_GROUP_W:(g + 1) * SGU_GROUP_W] = (
                _dot(sguw_ref[g], blk) + sgub_ref[:, g * SGU_GROUP_W:(g + 1) * SGU_GROUP_W])
    u = proj(O_CU, O_CV) * mix_ref[...] * jax.nn.silu(proj(O_CGATE, O_MERGE))
    yc = _dot(u.astype(BF16), wc_ref[...])

    m01_ref[0] = jax.nn.sigmoid(proj(O_MERGE, O_MERGE + 2 * D_MODEL)).astype(BF16)
    pc_ref[0] = (jax.nn.sigmoid(proj(O_MERGE + 2 * D_MODEL, IN_COLS)) * yc).astype(BF16)


def _pre_call(x, g, w_in, cs, sgug, sguw, sgub, wc):
    b, seq, _ = x.shape
    tok = _tiles(seq)[0]
    grid = (b, seq // tok)
    const2 = lambda bi, i: (0, 0)
    const3 = lambda bi, i: (0, 0, 0)
    tile = lambda bi, i: (bi, i, 0)
    head_tile = lambda bi, i: (bi, 0, i, 0)
    head_tile_t = lambda bi, i: (bi, 0, 0, i)
    single = pl.Buffered(1)
    in_specs = [
        pl.BlockSpec((1, tok, D_MODEL), tile),
        pl.BlockSpec((1, D_MODEL), const2),
        pl.BlockSpec((D_MODEL, IN_COLS), const2, pipeline_mode=single),
        pl.BlockSpec((tok, 2 * HEAD_W), lambda bi, i: (i, 0)),
        pl.BlockSpec((1, SGU_W), const2),
        pl.BlockSpec((SGU_GROUPS, CHUNK, CHUNK), const3),
        pl.BlockSpec((CHUNK, SGU_W), const2),
        pl.BlockSpec((SGU_W, D_MODEL), const2),
    ]
    out_shape = [
        jax.ShapeDtypeStruct((b, seq, CONV_W), BF16),
        jax.ShapeDtypeStruct((b, seq, CONV_W), BF16),
        jax.ShapeDtypeStruct((b, N_HEADS, HEAD_W, seq), BF16),
        jax.ShapeDtypeStruct((b, N_HEADS, seq, HEAD_W), BF16),
        jax.ShapeDtypeStruct((b, N_HEADS, seq // KEY_BLOCK, HEAD_W, KEY_BLOCK), BF16),
        jax.ShapeDtypeStruct((b, seq, ATTN_W), BF16),
        jax.ShapeDtypeStruct((b, seq, D_MODEL), BF16),
        jax.ShapeDtypeStruct((b, seq, 2 * D_MODEL), BF16),
    ]
    out_specs = [
        pl.BlockSpec((1, tok, CONV_W), tile),
        pl.BlockSpec((1, tok, CONV_W), tile),
        pl.BlockSpec((1, N_HEADS, HEAD_W, tok), head_tile_t),
        pl.BlockSpec((1, N_HEADS, tok, HEAD_W), head_tile),
        pl.BlockSpec((1, N_HEADS, tok // KEY_BLOCK, HEAD_W, KEY_BLOCK), lambda bi, i: (bi, 0, i, 0, 0)),
        pl.BlockSpec((1, tok, ATTN_W), tile),
        pl.BlockSpec((1, tok, D_MODEL), tile),
        pl.BlockSpec((1, tok, 2 * D_MODEL), tile),
    ]
    return pl.pallas_call(
        _pre_kernel,
        grid=grid,
        in_specs=in_specs,
        out_specs=out_specs,
        out_shape=out_shape,
        scratch_shapes=[pltpu.VMEM((tok, SGU_W), F32)],
        compiler_params=pltpu.CompilerParams(
            dimension_semantics=("arbitrary", "arbitrary"),
            vmem_limit_bytes=VMEM_LIMIT_BYTES),
        name="pre",
    )(x, g, w_in, cs, sgug, sguw, sgub, wc)


def _attn_kernel(lam_ref, g_ref, qt_ref, k_ref, vt_ref, o_ref,
                 q2_ref, *ring_acc_l, lam_init, tqi, tk):
    tq = qt_ref.shape[3]
    seq = k_ref.shape[2]
    nq = tq // tqi
    n = seq // tk
    items = nq * n
    r = 2 * tqi
    assert n & (n - 1) == 0
    log2_n = n.bit_length() - 1
    lam = (jnp.exp(jnp.sum(lam_ref[0:1, :] * lam_ref[1:2, :], axis=-1, keepdims=True))
           - jnp.exp(jnp.sum(lam_ref[2:3, :] * lam_ref[3:4, :], axis=-1, keepdims=True))
           + lam_init)

    feat = lax.broadcasted_iota(jnp.int32, (HEAD_W, 1), 0)
    is_map0 = (feat % HEAD_DIM) < (HEAD_DIM // 2)
    zero = jnp.zeros((), BF16)
    for j in range(nq):
        qt = qt_ref[0, 0, :, j * tqi:(j + 1) * tqi]
        q2_ref[j] = jnp.concatenate(
            [jnp.where(is_map0, qt, zero), jnp.where(is_map0, zero, qt)], axis=1)

    def split(t):
        return lax.shift_right_logical(t, log2_n), lax.bitwise_and(t, n - 1)

    def stage_q(t, s_ref):
        qi, kc = split(t)
        k0 = pl.multiple_of(kc * tk, tk)
        s = _dot(k_ref[0, 0, pl.ds(k0, tk), :], q2_ref[qi])
        s_ref[...] = s
        return jnp.max(s, axis=0, keepdims=True)

    def stage_s(t, mc, m, l, s_ref, p_ref):
        _, kc = split(t)
        m_prev = jnp.where(kc == 0, -jnp.inf, m)
        m_new = jnp.maximum(m_prev, mc)
        alpha = jnp.exp2(m_prev - m_new)
        p = jnp.exp2(s_ref[...] - m_new)
        l_new = alpha * l + jnp.sum(p, axis=0, keepdims=True)
        p_ref[...] = p.astype(BF16)
        return m_new, l_new, alpha

    def stage_p(t, alpha, l, acc, p_ref):
        qi, kc = split(t)
        kb0 = pl.multiple_of(kc * (tk // KEY_BLOCK), tk // KEY_BLOCK)
        vt = jnp.concatenate([vt_ref[0, 0, kb0 + i] for i in range(tk // KEY_BLOCK)], axis=1)
        acc = alpha * acc + _dot(vt, p_ref[...])
        acc_ref[qi] = acc
        l_ref[qi] = l
        return acc

    dist = PIPE_DIST
    nbuf = 2 * dist
    assert items % nbuf == 0
    s_refs = ring_acc_l[:nbuf]
    p_refs = ring_acc_l[nbuf:2 * nbuf]
    acc_ref, l_ref = ring_acc_l[2 * nbuf:]

    def step(tau, c, st, do_q=True, do_s=True, do_p=True):
        mc, alpha, lsnap, m, l, acc = st
        mc, alpha, lsnap = dict(mc), dict(alpha), dict(lsnap)
        cs = (c + dist) % nbuf
        if do_p:
            acc = stage_p(tau, alpha.pop(c), lsnap.pop(c), acc, p_refs[c])
        if do_s:
            m, l, alpha[cs] = stage_s(tau + dist, mc.pop(cs), m, l, s_refs[cs], p_refs[cs])
            lsnap[cs] = l
        if do_q:
            mc[c] = stage_q(tau + nbuf, s_refs[c])
        return mc, alpha, lsnap, m, l, acc

    row = lambda v: jnp.full((1, r), v, F32)
    st = ({}, {}, {}, row(-jnp.inf), row(0.0), jnp.zeros((HEAD_W, r), F32))
    for c in range(nbuf):
        st = step(c - nbuf, c, st, do_s=c >= dist, do_p=False)

    def pack(st):
        mc, alpha, lsnap, m, l, acc = st
        return (tuple(mc[c] for c in range(dist, nbuf)), tuple(alpha[c] for c in range(dist)),
                tuple(lsnap[c] for c in range(dist)), m, l, acc)

    def unpack(carry):
        mc, alpha, lsnap, m, l, acc = carry
        return ({dist + i: v for i, v in enumerate(mc)}, dict(enumerate(alpha)),
                dict(enumerate(lsnap)), m, l, acc)

    def body(u, carry):
        st = unpack(carry)
        for c in range(nbuf):
            st = step(u * nbuf + c, c, st)
        return pack(st)

    st = unpack(lax.fori_loop(0, items // nbuf - 1, body, pack(st)))
    for c in range(nbuf):
        st = step(items - nbuf + c, c, st, do_q=False, do_s=c < dist)

    def finalize(j, carry):
        a = acc_ref[j]
        lj = l_ref[j]
        o = a[:, :tqi] / lj[:, :tqi] - lam * (a[:, tqi:] / lj[:, tqi:])
        o = o * lax.rsqrt(jnp.mean(o * o, axis=0, keepdims=True) + RMS_EPS)
        o = o * g_ref[...] * (1.0 - lam_init)
        q0 = pl.multiple_of(j * tqi, tqi)
        o_ref[0, pl.ds(q0, tqi), :] = o.T.astype(o_ref.dtype)
        return carry

    lax.fori_loop(0, nq, finalize, 0)


def _attn_call(lam_vecs, subln_g, qt, k, vt, lam_init):
    b, _, seq, _ = k.shape
    _, tq, tqi, tk = _tiles(seq)
    nq = tq // tqi
    r = 2 * tqi
    grid = (b, N_HEADS, seq // tq)
    const2 = lambda bi, hd, i: (0, 0)
    return pl.pallas_call(
        functools.partial(_attn_kernel, lam_init=lam_init, tqi=tqi, tk=tk),
        grid=grid,
        in_specs=[
            pl.BlockSpec((4, HEAD_DIM), const2),
            pl.BlockSpec((HEAD_W, 1), const2),
            pl.BlockSpec((1, 1, HEAD_W, tq), lambda bi, hd, i: (bi, hd, 0, i)),
            pl.BlockSpec((1, 1, seq, HEAD_W), lambda bi, hd, i: (bi, hd, 0, 0)),
            pl.BlockSpec((1, 1, seq // KEY_BLOCK, HEAD_W, KEY_BLOCK),
                         lambda bi, hd, i: (bi, hd, 0, 0, 0)),
        ],
        out_specs=pl.BlockSpec((1, tq, HEAD_W), lambda bi, hd, i: (bi, i, hd)),
        out_shape=jax.ShapeDtypeStruct((b, seq, ATTN_W), BF16),
        scratch_shapes=[
            pltpu.VMEM((nq, HEAD_W, r), BF16),
            *[pltpu.VMEM((tk, r), F32)] * (2 * PIPE_DIST),
            *[pltpu.VMEM((tk, r), BF16)] * (2 * PIPE_DIST),
            pltpu.VMEM((nq, HEAD_W, r), F32),
            pltpu.VMEM((nq, 1, r), F32),
        ],
        compiler_params=pltpu.CompilerParams(
            dimension_semantics=("arbitrary", "arbitrary", "arbitrary"),
            vmem_limit_bytes=VMEM_LIMIT_BYTES),
        name="attn",
    )(lam_vecs, subln_g, qt, k, vt)


def _post_kernel(x_ref, aprev_ref, acur_ref, anext_ref, ga_ref, o_ref, gb_ref, pc_ref, m01_ref,
                 cw_ref, cb_ref, cg_ref, wa_ref, wb_ref, wo_ref, fg_ref,
                 y_ref, abuf_ref, conv_ref, *, final, rows):
    tok = x_ref.shape[1]
    i = pl.program_id(1)
    last = pl.num_programs(1) - 1
    prev = aprev_ref[0].astype(F32)
    nxt = anext_ref[0].astype(F32)
    abuf_ref[0:CONV_HALO, :] = jnp.where(i > 0, prev, 0.0)
    abuf_ref[CONV_HALO:CONV_HALO + tok, :] = acur_ref[0].astype(F32)
    abuf_ref[CONV_HALO + tok:, :] = jnp.where(i < last, nxt, 0.0)

    shift = CONV_HALO - CONV_K // 2
    win_rows = rows + 2 * CONV_HALO

    def conv_rows(r, carry):
        r0 = pl.multiple_of(r * rows, rows)
        win = abuf_ref[pl.ds(r0, win_rows), :]
        acc = jnp.zeros((rows, CONV_W), F32)
        for s in range(SUBLANES):
            ws = win if s == 0 else pltpu.roll(win, win_rows - s, 0)
            for a in range(win_rows // SUBLANES):
                k = a * SUBLANES + s - shift
                if 0 <= k < CONV_K:
                    acc = acc + cw_ref[k:k + 1, :] * ws[a * SUBLANES:a * SUBLANES + rows]
        conv_ref[pl.ds(r0, rows), :] = acc
        return carry

    lax.fori_loop(0, tok // rows, conv_rows, 0)

    a = jax.nn.silu(_rms(conv_ref[...] + cb_ref[...], cg_ref[...]))
    ya = _dot((a * ga_ref[0].astype(F32)).astype(BF16), wa_ref[...])
    yb = _dot((o_ref[0].astype(F32) * gb_ref[0].astype(F32)).astype(BF16), wb_ref[...])
    y = (m01_ref[0, :, :D_MODEL].astype(F32) * ya + m01_ref[0, :, D_MODEL:].astype(F32) * yb
         + pc_ref[0].astype(F32))
    out = x_ref[0] + _dot(y.astype(BF16), wo_ref[...])
    if final:
        out = _rms(out, fg_ref[...])
    y_ref[0] = out


def _post_call(x, aglu, ga, o, gb, pc, m01, cw, cb, cg, wa, wb, wo, fg, final):
    b, seq, _ = x.shape
    tok = _tiles(seq)[0]
    halo_blocks = tok // CONV_HALO
    n_halo = seq // CONV_HALO
    grid = (b, seq // tok)
    tile = lambda bi, i: (bi, i, 0)
    const2 = lambda bi, i: (0, 0)
    in_specs = [
        pl.BlockSpec((1, tok, D_MODEL), tile),
        pl.BlockSpec((1, CONV_HALO, CONV_W),
                     lambda bi, i: (bi, jnp.maximum(i * halo_blocks - 1, 0), 0)),
        pl.BlockSpec((1, tok, CONV_W), tile),
        pl.BlockSpec((1, CONV_HALO, CONV_W),
                     lambda bi, i: (bi, jnp.minimum((i + 1) * halo_blocks, n_halo - 1), 0)),
        pl.BlockSpec((1, tok, CONV_W), tile),
        pl.BlockSpec((1, tok, ATTN_W), tile),
        pl.BlockSpec((1, tok, ATTN_W), tile),
        pl.BlockSpec((1, tok, D_MODEL), tile),
        pl.BlockSpec((1, tok, 2 * D_MODEL), tile),
        pl.BlockSpec((CONV_K, CONV_W), const2),
        pl.BlockSpec((1, CONV_W), const2),
        pl.BlockSpec((1, CONV_W), const2),
        pl.BlockSpec((CONV_W, D_MODEL), const2),
        pl.BlockSpec((ATTN_W, D_MODEL), const2),
        pl.BlockSpec((D_MODEL, D_MODEL), const2),
        pl.BlockSpec((1, D_MODEL), const2),
    ]
    return pl.pallas_call(
        functools.partial(_post_kernel, final=final, rows=32),
        grid=grid,
        in_specs=in_specs,
        out_specs=pl.BlockSpec((1, tok, D_MODEL), tile),
        out_shape=jax.ShapeDtypeStruct((b, seq, D_MODEL), F32),
        scratch_shapes=[pltpu.VMEM((tok + 2 * CONV_HALO, CONV_W), F32),
                        pltpu.VMEM((tok, CONV_W), F32)],
        compiler_params=pltpu.CompilerParams(
            dimension_semantics=("arbitrary", "arbitrary"),
            vmem_limit_bytes=VMEM_LIMIT_BYTES),
        name="post",
    )(x, aglu, aglu, aglu, ga, o, gb, pc, m01, cw, cb, cg, wa, wb, wo, fg)


def _permute_heads(w):
    k = w.shape[0]
    w = w.reshape(k, N_HEADS, 2, 2, HEAD_DIM // 2)
    return w.transpose(0, 1, 3, 2, 4).reshape(k, ATTN_W)


def _rotary_table(seq):
    half = HEAD_DIM // 2
    inv = ROPE_THETA ** (-jnp.arange(half, dtype=F32) / half)
    ang = jnp.arange(seq, dtype=jnp.int32).astype(F32)[:, None] * inv[None, :]
    cos = jnp.tile(jnp.cos(ang), (1, 4))
    sin = jnp.tile(jnp.sin(ang), (1, 4))
    sign = jnp.where(jnp.arange(HEAD_W) < HEAD_DIM, -1.0, 1.0).astype(F32)
    return jnp.concatenate([cos, sin * sign[None, :]], axis=1)


def kernel(x_prompt, x_sample, norm_g, w_in, conv_w, conv_b, conv_norm_g, w_proj_a,
           lam_q1, lam_k1, lam_q2, lam_k2, subln_g, w_proj_b,
           sgu_norm_g, sgu_w, sgu_b, w_proj_c, w_out, final_g):
    w_in_p = jnp.concatenate(
        [w_in[:, :, :O_Q], jax.vmap(_permute_heads)(w_in[:, :, O_Q:O_K]),
         jax.vmap(_permute_heads)(w_in[:, :, O_K:O_V]), w_in[:, :, O_V:]], axis=-1).astype(BF16)
    sgu_w_b = sgu_w.astype(BF16)
    sgu_bias = jnp.repeat(jnp.swapaxes(sgu_b, 1, 2), SGU_GROUP_W, axis=2)
    wc_b = w_proj_c.astype(BF16)
    wa_b = w_proj_a.astype(BF16)
    wb_b = w_proj_b.astype(BF16)
    wo_b = w_out.astype(BF16)
    lam_vecs = jnp.stack([lam_q1, lam_k1, lam_q2, lam_k2], axis=1).astype(F32)
    cs = _rotary_table(max(x_prompt.shape[1], x_sample.shape[1]))

    outs = []
    for x in (x_prompt, x_sample):
        for l in range(DEPTH):
            aglu, ga, qt, k, vt, gb, pc, m01 = _pre_call(
                x, norm_g[l][None], w_in_p[l], cs, sgu_norm_g[l][None], sgu_w_b[l], sgu_bias[l],
                wc_b[l])
            o = _attn_call(lam_vecs[l], subln_g[l][:, None], qt, k, vt, _lambda_init(l))
            x = _post_call(x, aglu, ga, o, gb, pc, m01, conv_w[l], conv_b[l][None],
                           conv_norm_g[l][None], wa_b[l], wb_b[l], wo_b[l], final_g[None],
                           final=(l == DEPTH - 1))
        outs.append(x)
    return tuple(outs)
```

```python
import functools
import math

import jax
import jax.numpy as jnp
from jax import lax
from jax.experimental import pallas as pl
from jax.experimental.pallas import tpu as pltpu

D_MODEL = 1024
DEPTH = 2
CONV_W = 512
CONV_K = 31
CONV_HALO = 16
SUBLANES = 8
N_HEADS = 8
HEAD_DIM = 64
HEAD_W = 2 * HEAD_DIM
ATTN_W = N_HEADS * HEAD_W
ROPE_THETA = 10000.0
SGU_W = 512
SGU_GROUPS = 4
SGU_GROUP_W = SGU_W // SGU_GROUPS
CHUNK = 128
N_BRANCH = 3
RMS_EPS = 1e-6
KEY_BLOCK = 128
BODY_STEPS = 2
BOUND_SLACK = 1.01
MIN_DENOMINATOR = 2.0 ** -64

_COLS = (2 * CONV_W, CONV_W, ATTN_W, ATTN_W, ATTN_W, ATTN_W, SGU_W, SGU_W, SGU_W, N_BRANCH * D_MODEL)
_OFF = [0]
for _c in _COLS:
    _OFF.append(_OFF[-1] + _c)
(O_AIN, O_AGATE, O_Q, O_K, O_V, O_BGATE, O_CU, O_CV, O_CGATE, O_MERGE, IN_COLS) = _OFF

F32 = jnp.float32
BF16 = jnp.bfloat16

VMEM_LIMIT_BYTES = 56 * 1024 * 1024


def _lambda_init(layer_idx):
    return 0.8 - 0.6 * math.exp(-0.3 * layer_idx)


def _rms(x, g):
    return x * lax.rsqrt(jnp.mean(x * x, axis=-1, keepdims=True) + RMS_EPS) * g


def _dot(a, b):
    return jnp.dot(a, b, preferred_element_type=F32)


def _tiles(seq):
    tok = 256
    tq = min(seq, 2048) if seq <= 2048 else 1024
    tqi = 128
    tk = min(seq, 8192)
    return tok, tq, tqi, tk


def _pre_kernel(x_ref, g_ref, w_ref, cs_ref, sgug_ref, sguw_ref, sgub_ref, wc_ref,
                aglu_ref, ga_ref, qt_ref, k_ref, vt_ref, gb_ref, pc_ref, m01_ref,
                mix_ref):
    tok = x_ref.shape[1]
    h = _rms(x_ref[0], g_ref[...]).astype(BF16)

    def proj(c0, c1):
        return _dot(h, w_ref[:, c0:c1])

    za = proj(O_AIN, O_AGATE)
    aglu_ref[0] = (za[:, :CONV_W] * jax.nn.sigmoid(za[:, CONV_W:])).astype(BF16)
    ga_ref[0] = jax.nn.silu(proj(O_AGATE, O_Q)).astype(BF16)

    cos = cs_ref[:, :HEAD_W]
    sin = cs_ref[:, HEAD_W:]
    scale = math.log2(math.e) / math.sqrt(HEAD_DIM)
    zq = proj(O_Q, O_K)
    for hd in range(N_HEADS):
        s = zq[:, hd * HEAD_W:(hd + 1) * HEAD_W]
        qt_ref[0, hd] = ((s * cos + pltpu.roll(s, HEAD_DIM, 1) * sin) * scale).T.astype(BF16)
    zk = proj(O_K, O_V)
    for hd in range(N_HEADS):
        s = zk[:, hd * HEAD_W:(hd + 1) * HEAD_W]
        k_ref[0, hd] = (s * cos + pltpu.roll(s, HEAD_DIM, 1) * sin).astype(BF16)
    zv = proj(O_V, O_BGATE)
    for hd in range(N_HEADS):
        for c in range(tok // KEY_BLOCK):
            vt_ref[0, hd, c] = zv[c * KEY_BLOCK:(c + 1) * KEY_BLOCK,
                                  hd * HEAD_W:(hd + 1) * HEAD_W].T.astype(BF16)
    gb_ref[0] = jax.nn.silu(proj(O_BGATE, O_CU)).astype(BF16)

    cv = _rms(proj(O_CV, O_CGATE), sgug_ref[...]).astype(BF16)
    for n in range(tok // CHUNK):
        for g in range(SGU_GROUPS):
            blk = cv[n * CHUNK:(n + 1) * CHUNK, g * SGU_GROUP_W:(g + 1) * SGU_GROUP_W]
            mix_ref[n * CHUNK:(n + 1) * CHUNK, g * SGU_GROUP_W:(g + 1) * SGU_GROUP_W] = (
                _dot(sguw_ref[g], blk) + sgub_ref[:, g * SGU_GROUP_W:(g + 1) * SGU_GROUP_W])
    u = proj(O_CU, O_CV) * mix_ref[...] * jax.nn.silu(proj(O_CGATE, O_MERGE))
    yc = _dot(u.astype(BF16), wc_ref[...])

    m01_ref[0] = jax.nn.sigmoid(proj(O_MERGE, O_MERGE + 2 * D_MODEL)).astype(BF16)
    pc_ref[0] = (jax.nn.sigmoid(proj(O_MERGE + 2 * D_MODEL, IN_COLS)) * yc).astype(BF16)


def _pre_call(x, g, w_in, cs, sgug, sguw, sgub, wc):
    b, seq, _ = x.shape
    tok = _tiles(seq)[0]
    grid = (b, seq // tok)
    const2 = lambda bi, i: (0, 0)
    const3 = lambda bi, i: (0, 0, 0)
    tile = lambda bi, i: (bi, i, 0)
    head_tile = lambda bi, i: (bi, 0, i, 0)
    head_tile_t = lambda bi, i: (bi, 0, 0, i)
    single = pl.Buffered(1)
    in_specs = [
        pl.BlockSpec((1, tok, D_MODEL), tile),
        pl.BlockSpec((1, D_MODEL), const2),
        pl.BlockSpec((D_MODEL, IN_COLS), const2, pipeline_mode=single),
        pl.BlockSpec((tok, 2 * HEAD_W), lambda bi, i: (i, 0)),
        pl.BlockSpec((1, SGU_W), const2),
        pl.BlockSpec((SGU_GROUPS, CHUNK, CHUNK), const3),
        pl.BlockSpec((CHUNK, SGU_W), const2),
        pl.BlockSpec((SGU_W, D_MODEL), const2),
    ]
    out_shape = [
        jax.ShapeDtypeStruct((b, seq, CONV_W), BF16),
        jax.ShapeDtypeStruct((b, seq, CONV_W), BF16),
        jax.ShapeDtypeStruct((b, N_HEADS, HEAD_W, seq), BF16),
        jax.ShapeDtypeStruct((b, N_HEADS, seq, HEAD_W), BF16),
        jax.ShapeDtypeStruct((b, N_HEADS, seq // KEY_BLOCK, HEAD_W, KEY_BLOCK), BF16),
        jax.ShapeDtypeStruct((b, seq, ATTN_W), BF16),
        jax.ShapeDtypeStruct((b, seq, D_MODEL), BF16),
        jax.ShapeDtypeStruct((b, seq, 2 * D_MODEL), BF16),
    ]
    out_specs = [
        pl.BlockSpec((1, tok, CONV_W), tile),
        pl.BlockSpec((1, tok, CONV_W), tile),
        pl.BlockSpec((1, N_HEADS, HEAD_W, tok), head_tile_t),
        pl.BlockSpec((1, N_HEADS, tok, HEAD_W), head_tile),
        pl.BlockSpec((1, N_HEADS, tok // KEY_BLOCK, HEAD_W, KEY_BLOCK), lambda bi, i: (bi, 0, i, 0, 0)),
        pl.BlockSpec((1, tok, ATTN_W), tile),
        pl.BlockSpec((1, tok, D_MODEL), tile),
        pl.BlockSpec((1, tok, 2 * D_MODEL), tile),
    ]
    return pl.pallas_call(
        _pre_kernel,
        grid=grid,
        in_specs=in_specs,
        out_specs=out_specs,
        out_shape=out_shape,
        scratch_shapes=[pltpu.VMEM((tok, SGU_W), F32)],
        compiler_params=pltpu.CompilerParams(
            dimension_semantics=("arbitrary", "arbitrary"),
            vmem_limit_bytes=VMEM_LIMIT_BYTES),
        name="pre",
    )(x, g, w_in, cs, sgug, sguw, sgub, wc)


def _attn_kernel(lam_ref, g_ref, qt_ref, k_ref, vt_ref, o_ref,
                 q2_ref, mref_ref, kmax_ref, p0_ref, p1_ref, acc_ref, l_ref, *, lam_init, tqi, tk):
    tq = qt_ref.shape[3]
    seq = k_ref.shape[2]
    nq = tq // tqi
    n = seq // tk
    items = nq * n
    r = 2 * tqi
    assert n & (n - 1) == 0 and items % 2 == 0 and items >= 2
    log2_n = n.bit_length() - 1
    lam = (jnp.exp(jnp.sum(lam_ref[0:1, :] * lam_ref[1:2, :], axis=-1, keepdims=True))
           - jnp.exp(jnp.sum(lam_ref[2:3, :] * lam_ref[3:4, :], axis=-1, keepdims=True))
           + lam_init)

    feat = lax.broadcasted_iota(jnp.int32, (HEAD_W, 1), 0)
    is_map0 = (feat % HEAD_DIM) < (HEAD_DIM // 2)

    @pl.when(pl.program_id(2) == 0)
    def _():
        col_map0 = lax.broadcasted_iota(jnp.int32, (1, r), 1) < tqi
        sel = jnp.where(is_map0 == col_map0, 1.0, 0.0).astype(BF16)

        def chunk_max(kc, best):
            k0 = pl.multiple_of(kc * tk, tk)
            kf = k_ref[0, 0, pl.ds(k0, tk), :].astype(F32)
            return jnp.maximum(best, jnp.max(_dot((kf * kf).astype(BF16), sel),
                                             axis=0, keepdims=True))

        kmax_ref[...] = lax.fori_loop(0, n, chunk_max, jnp.zeros((1, r), F32))

    zero = jnp.zeros((), BF16)
    for j in range(nq):
        qt = qt_ref[0, 0, :, j * tqi:(j + 1) * tqi]
        q2 = jnp.concatenate([jnp.where(is_map0, qt, zero), jnp.where(is_map0, zero, qt)], axis=1)
        q2_ref[j] = q2
        q2f = q2.astype(F32)
        qn2 = jnp.sum(q2f * q2f, axis=0, keepdims=True)
        mref_ref[j] = jnp.sqrt(qn2 * kmax_ref[...]) * BOUND_SLACK

    def split(t):
        return lax.shift_right_logical(t, log2_n), lax.bitwise_and(t, n - 1)

    def k_chunk(kc):
        return k_ref[0, 0, pl.ds(pl.multiple_of(kc * tk, tk), tk), :]

    def vt_chunk(kc):
        kb0 = pl.multiple_of(kc * (tk // KEY_BLOCK), tk // KEY_BLOCK)
        return jnp.concatenate([vt_ref[0, 0, kb0 + i] for i in range(tk // KEY_BLOCK)], axis=1)

    def stage_qs(t, l, p_ref):
        qi, kc = split(t)
        p = jnp.exp2(_dot(k_chunk(kc), q2_ref[qi]) - mref_ref[qi])
        p_ref[...] = p.astype(BF16)
        l = jnp.where(kc == 0, 0.0, l) + jnp.sum(p, axis=0, keepdims=True)
        return l

    def stage_p(t, l, acc, p_ref):
        qi, kc = split(t)
        acc = jnp.where(kc == 0, 0.0, acc) + _dot(vt_chunk(kc), p_ref[...])
        acc_ref[qi] = acc
        l_ref[qi] = l
        return acc

    p_refs = (p0_ref, p1_ref)
    l0 = stage_qs(0, jnp.zeros((1, r), F32), p0_ref)

    def step(tau, c, carry):
        l_prev, acc = carry
        acc = stage_p(tau, l_prev, acc, p_refs[c])
        return stage_qs(tau + 1, l_prev, p_refs[1 - c]), acc

    def body(u, carry):
        for j in range(BODY_STEPS):
            carry = step(u * BODY_STEPS + j, j % 2, carry)
        return carry

    assert BODY_STEPS % 2 == 0
    trips = (items - 1) // BODY_STEPS
    carry = lax.fori_loop(0, trips, body, (l0, jnp.zeros((HEAD_W, r), F32)))
    for tau in range(trips * BODY_STEPS, items - 1):
        carry = step(tau, tau % 2, carry)
    stage_p(items - 1, *carry, p_refs[(items - 1) % 2])

    @pl.when(jnp.logical_not(jnp.min(l_ref[...]) >= MIN_DENOMINATOR))
    def _():
        def exact_block(j, carry):
            def chunk(kc, mla):
                m, l, acc = mla
                s = _dot(k_chunk(kc), q2_ref[j])
                m_new = jnp.maximum(m, jnp.max(s, axis=0, keepdims=True))
                alpha = jnp.exp2(m - m_new)
                p = jnp.exp2(s - m_new)
                l = alpha * l + jnp.sum(p, axis=0, keepdims=True)
                acc = alpha * acc + _dot(vt_chunk(kc), p.astype(BF16))
                return m_new, l, acc

            init = (jnp.full((1, r), -jnp.inf, F32), jnp.zeros((1, r), F32),
                    jnp.zeros((HEAD_W, r), F32))
            _, l, acc = lax.fori_loop(0, n, chunk, init)
            acc_ref[j] = acc
            l_ref[j] = l
            return carry

        lax.fori_loop(0, nq, exact_block, 0)

    gain = g_ref[...] * (1.0 - lam_init)
    for j in range(nq):
        a = acc_ref[j]
        lj = l_ref[j]
        o = a[:, :tqi] / lj[:, :tqi] - lam * (a[:, tqi:] / lj[:, tqi:])
        o = o * lax.rsqrt(jnp.mean(o * o, axis=0, keepdims=True) + RMS_EPS) * gain
        o_ref[0, j * tqi:(j + 1) * tqi, :] = o.T.astype(o_ref.dtype)


def _attn_call(lam_vecs, subln_g, qt, k, vt, lam_init):
    b, _, seq, _ = k.shape
    _, tq, tqi, tk = _tiles(seq)
    nq = tq // tqi
    r = 2 * tqi
    grid = (b, N_HEADS, seq // tq)
    const2 = lambda bi, hd, i: (0, 0)
    return pl.pallas_call(
        functools.partial(_attn_kernel, lam_init=lam_init, tqi=tqi, tk=tk),
        grid=grid,
        in_specs=[
            pl.BlockSpec((4, HEAD_DIM), const2),
            pl.BlockSpec((HEAD_W, 1), const2),
            pl.BlockSpec((1, 1, HEAD_W, tq), lambda bi, hd, i: (bi, hd, 0, i)),
            pl.BlockSpec((1, 1, seq, HEAD_W), lambda bi, hd, i: (bi, hd, 0, 0)),
            pl.BlockSpec((1, 1, seq // KEY_BLOCK, HEAD_W, KEY_BLOCK),
                         lambda bi, hd, i: (bi, hd, 0, 0, 0)),
        ],
        out_specs=pl.BlockSpec((1, tq, HEAD_W), lambda bi, hd, i: (bi, i, hd)),
        out_shape=jax.ShapeDtypeStruct((b, seq, ATTN_W), BF16),
        scratch_shapes=[
            pltpu.VMEM((nq, HEAD_W, r), BF16),
            pltpu.VMEM((nq, 1, r), F32),
            pltpu.VMEM((1, r), F32),
            pltpu.VMEM((tk, r), BF16), pltpu.VMEM((tk, r), BF16),
            pltpu.VMEM((nq, HEAD_W, r), F32),
            pltpu.VMEM((nq, 1, r), F32),
        ],
        compiler_params=pltpu.CompilerParams(
            dimension_semantics=("arbitrary", "arbitrary", "arbitrary"),
            vmem_limit_bytes=VMEM_LIMIT_BYTES),
        name="attn",
    )(lam_vecs, subln_g, qt, k, vt)


def _post_kernel(x_ref, aprev_ref, acur_ref, anext_ref, ga_ref, o_ref, gb_ref, pc_ref, m01_ref,
                 cw_ref, cb_ref, cg_ref, wa_ref, wb_ref, wo_ref, fg_ref,
                 y_ref, abuf_ref, conv_ref, *, final, rows):
    tok = x_ref.shape[1]
    i = pl.program_id(1)
    last = pl.num_programs(1) - 1
    prev = aprev_ref[0].astype(F32)
    nxt = anext_ref[0].astype(F32)
    abuf_ref[0:CONV_HALO, :] = jnp.where(i > 0, prev, 0.0)
    abuf_ref[CONV_HALO:CONV_HALO + tok, :] = acur_ref[0].astype(F32)
    abuf_ref[CONV_HALO + tok:, :] = jnp.where(i < last, nxt, 0.0)

    shift = CONV_HALO - CONV_K // 2
    win_rows = rows + 2 * CONV_HALO

    def conv_rows(r, carry):
        r0 = pl.multiple_of(r * rows, rows)
        win = abuf_ref[pl.ds(r0, win_rows), :]
        acc = jnp.zeros((rows, CONV_W), F32)
        for s in range(SUBLANES):
            ws = win if s == 0 else pltpu.roll(win, win_rows - s, 0)
            for a in range(win_rows // SUBLANES):
                k = a * SUBLANES + s - shift
                if 0 <= k < CONV_K:
                    acc = acc + cw_ref[k:k + 1, :] * ws[a * SUBLANES:a * SUBLANES + rows]
        conv_ref[pl.ds(r0, rows), :] = acc
        return carry

    lax.fori_loop(0, tok // rows, conv_rows, 0)

    a = jax.nn.silu(_rms(conv_ref[...] + cb_ref[...], cg_ref[...]))
    ya = _dot((a * ga_ref[0].astype(F32)).astype(BF16), wa_ref[...])
    yb = _dot((o_ref[0].astype(F32) * gb_ref[0].astype(F32)).astype(BF16), wb_ref[...])
    y = (m01_ref[0, :, :D_MODEL].astype(F32) * ya + m01_ref[0, :, D_MODEL:].astype(F32) * yb
         + pc_ref[0].astype(F32))
    out = x_ref[0] + _dot(y.astype(BF16), wo_ref[...])
    if final:
        out = _rms(out, fg_ref[...])
    y_ref[0] = out


def _post_call(x, aglu, ga, o, gb, pc, m01, cw, cb, cg, wa, wb, wo, fg, final):
    b, seq, _ = x.shape
    tok = _tiles(seq)[0]
    halo_blocks = tok // CONV_HALO
    n_halo = seq // CONV_HALO
    grid = (b, seq // tok)
    tile = lambda bi, i: (bi, i, 0)
    const2 = lambda bi, i: (0, 0)
    in_specs = [
        pl.BlockSpec((1, tok, D_MODEL), tile),
        pl.BlockSpec((1, CONV_HALO, CONV_W),
                     lambda bi, i: (bi, jnp.maximum(i * halo_blocks - 1, 0), 0)),
        pl.BlockSpec((1, tok, CONV_W), tile),
        pl.BlockSpec((1, CONV_HALO, CONV_W),
                     lambda bi, i: (bi, jnp.minimum((i + 1) * halo_blocks, n_halo - 1), 0)),
        pl.BlockSpec((1, tok, CONV_W), tile),
        pl.BlockSpec((1, tok, ATTN_W), tile),
        pl.BlockSpec((1, tok, ATTN_W), tile),
        pl.BlockSpec((1, tok, D_MODEL), tile),
        pl.BlockSpec((1, tok, 2 * D_MODEL), tile),
        pl.BlockSpec((CONV_K, CONV_W), const2),
        pl.BlockSpec((1, CONV_W), const2),
        pl.BlockSpec((1, CONV_W), const2),
        pl.BlockSpec((CONV_W, D_MODEL), const2),
        pl.BlockSpec((ATTN_W, D_MODEL), const2),
        pl.BlockSpec((D_MODEL, D_MODEL), const2),
        pl.BlockSpec((1, D_MODEL), const2),
    ]
    return pl.pallas_call(
        functools.partial(_post_kernel, final=final, rows=32),
        grid=grid,
        in_specs=in_specs,
        out_specs=pl.BlockSpec((1, tok, D_MODEL), tile),
        out_shape=jax.ShapeDtypeStruct((b, seq, D_MODEL), F32),
        scratch_shapes=[pltpu.VMEM((tok + 2 * CONV_HALO, CONV_W), F32),
                        pltpu.VMEM((tok, CONV_W), F32)],
        compiler_params=pltpu.CompilerParams(
            dimension_semantics=("arbitrary", "arbitrary"),
            vmem_limit_bytes=VMEM_LIMIT_BYTES),
        name="post",
    )(x, aglu, aglu, aglu, ga, o, gb, pc, m01, cw, cb, cg, wa, wb, wo, fg)


def _permute_heads(w):
    k = w.shape[0]
    w = w.reshape(k, N_HEADS, 2, 2, HEAD_DIM // 2)
    return w.transpose(0, 1, 3, 2, 4).reshape(k, ATTN_W)


def _rotary_table(seq):
    half = HEAD_DIM // 2
    inv = ROPE_THETA ** (-jnp.arange(half, dtype=F32) / half)
    ang = jnp.arange(seq, dtype=jnp.int32).astype(F32)[:, None] * inv[None, :]
    cos = jnp.tile(jnp.cos(ang), (1, 4))
    sin = jnp.tile(jnp.sin(ang), (1, 4))
    sign = jnp.where(jnp.arange(HEAD_W) < HEAD_DIM, -1.0, 1.0).astype(F32)
    return jnp.concatenate([cos, sin * sign[None, :]], axis=1)


def kernel(x_prompt, x_sample, norm_g, w_in, conv_w, conv_b, conv_norm_g, w_proj_a,
           lam_q1, lam_k1, lam_q2, lam_k2, subln_g, w_proj_b,
           sgu_norm_g, sgu_w, sgu_b, w_proj_c, w_out, final_g):
    w_in_p = jnp.concatenate(
        [w_in[:, :, :O_Q], jax.vmap(_permute_heads)(w_in[:, :, O_Q:O_K]),
         jax.vmap(_permute_heads)(w_in[:, :, O_K:O_V]), w_in[:, :, O_V:]], axis=-1).astype(BF16)
    sgu_w_b = sgu_w.astype(BF16)
    sgu_bias = jnp.repeat(jnp.swapaxes(sgu_b, 1, 2), SGU_GROUP_W, axis=2)
    wc_b = w_proj_c.astype(BF16)
    wa_b = w_proj_a.astype(BF16)
    wb_b = w_proj_b.astype(BF16)
    wo_b = w_out.astype(BF16)
    lam_vecs = jnp.stack([lam_q1, lam_k1, lam_q2, lam_k2], axis=1).astype(F32)
    cs = _rotary_table(max(x_prompt.shape[1], x_sample.shape[1]))

    outs = []
    for x in (x_prompt, x_sample):
        for l in range(DEPTH):
            aglu, ga, qt, k, vt, gb, pc, m01 = _pre_call(
                x, norm_g[l][None], w_in_p[l], cs, sgu_norm_g[l][None], sgu_w_b[l], sgu_bias[l],
                wc_b[l])
            o = _attn_call(lam_vecs[l], subln_g[l][:, None], qt, k, vt, _lambda_init(l))
            x = _post_call(x, aglu, ga, o, gb, pc, m01, conv_w[l], conv_b[l][None],
                           conv_norm_g[l][None], wa_b[l], wb_b[l], wo_b[l], final_g[None],
                           final=(l == DEPTH - 1))
        outs.append(x)
    return tuple(outs)
```

```python
import functools
import math

import jax
import jax.numpy as jnp
from jax import lax
from jax.experimental import pallas as pl
from jax.experimental.pallas import tpu as pltpu

D_MODEL = 1024
DEPTH = 2
CONV_W = 512
CONV_K = 31
CONV_HALO = 16
SUBLANES = 8
N_HEADS = 8
HEAD_DIM = 64
HEAD_W = 2 * HEAD_DIM
ATTN_W = N_HEADS * HEAD_W
ROPE_THETA = 10000.0
SGU_W = 512
SGU_GROUPS = 4
SGU_GROUP_W = SGU_W // SGU_GROUPS
CHUNK = 128
N_BRANCH = 3
RMS_EPS = 1e-6
KEY_BLOCK = 128
BODY_STEPS = 2
BOUND_SLACK = 1.01
MIN_DENOMINATOR = 2.0 ** -64

_COLS = (2 * CONV_W, CONV_W, ATTN_W, ATTN_W, ATTN_W, ATTN_W, SGU_W, SGU_W, SGU_W, N_BRANCH * D_MODEL)
_OFF = [0]
for _c in _COLS:
    _OFF.append(_OFF[-1] + _c)
(O_AIN, O_AGATE, O_Q, O_K, O_V, O_BGATE, O_CU, O_CV, O_CGATE, O_MERGE, IN_COLS) = _OFF

F32 = jnp.float32
BF16 = jnp.bfloat16

VMEM_LIMIT_BYTES = 56 * 1024 * 1024


def _lambda_init(layer_idx):
    return 0.8 - 0.6 * math.exp(-0.3 * layer_idx)


def _rms(x, g):
    return x * lax.rsqrt(jnp.mean(x * x, axis=-1, keepdims=True) + RMS_EPS) * g


def _dot(a, b):
    return jnp.dot(a, b, preferred_element_type=F32)


def _tiles(seq):
    tok = 256
    tq = min(seq, 4096)
    tqi = 256 if seq <= 2048 else 128
    tk = min(seq, 8192)
    heads = 4 if seq <= 2048 else 1
    return tok, tq, tqi, tk, heads


def _pre_kernel(x_ref, g_ref, w_ref, cs_ref, sgug_ref, sguw_ref, sgub_ref, wc_ref,
                aglu_ref, ga_ref, qt_ref, k_ref, vt_ref, gb_ref, pc_ref, m01_ref,
                mix_ref):
    tok = x_ref.shape[1]
    h = _rms(x_ref[0], g_ref[...]).astype(BF16)

    def proj(c0, c1):
        return _dot(h, w_ref[:, c0:c1])

    za = proj(O_AIN, O_AGATE)
    aglu_ref[0] = (za[:, :CONV_W] * jax.nn.sigmoid(za[:, CONV_W:])).astype(BF16)
    ga_ref[0] = jax.nn.silu(proj(O_AGATE, O_Q)).astype(BF16)

    cos = cs_ref[:, :HEAD_W]
    sin = cs_ref[:, HEAD_W:]
    scale = math.log2(math.e) / math.sqrt(HEAD_DIM)
    zq = proj(O_Q, O_K)
    for hd in range(N_HEADS):
        s = zq[:, hd * HEAD_W:(hd + 1) * HEAD_W]
        qt_ref[0, hd] = ((s * cos + pltpu.roll(s, HEAD_DIM, 1) * sin) * scale).T.astype(BF16)
    zk = proj(O_K, O_V)
    for hd in range(N_HEADS):
        s = zk[:, hd * HEAD_W:(hd + 1) * HEAD_W]
        k_ref[0, hd] = (s * cos + pltpu.roll(s, HEAD_DIM, 1) * sin).astype(BF16)
    zv = proj(O_V, O_BGATE)
    for hd in range(N_HEADS):
        for c in range(tok // KEY_BLOCK):
            vt_ref[0, hd, c] = zv[c * KEY_BLOCK:(c + 1) * KEY_BLOCK,
                                  hd * HEAD_W:(hd + 1) * HEAD_W].T.astype(BF16)
    gb_ref[0] = jax.nn.silu(proj(O_BGATE, O_CU)).astype(BF16)

    cv = _rms(proj(O_CV, O_CGATE), sgug_ref[...]).astype(BF16)
    for n in range(tok // CHUNK):
        for g in range(SGU_GROUPS):
            blk = cv[n * CHUNK:(n + 1) * CHUNK, g * SGU_GROUP_W:(g + 1) * SGU_GROUP_W]
            mix_ref[n * CHUNK:(n + 1) * CHUNK, g * SGU_GROUP_W:(g + 1) * SGU_GROUP_W] = (
                _dot(sguw_ref[g], blk) + sgub_ref[:, g * SGU_GROUP_W:(g + 1) * SGU_GROUP_W])
    u = proj(O_CU, O_CV) * mix_ref[...] * jax.nn.silu(proj(O_CGATE, O_MERGE))
    yc = _dot(u.astype(BF16), wc_ref[...])

    m01_ref[0] = jax.nn.sigmoid(proj(O_MERGE, O_MERGE + 2 * D_MODEL)).astype(BF16)
    pc_ref[0] = (jax.nn.sigmoid(proj(O_MERGE + 2 * D_MODEL, IN_COLS)) * yc).astype(BF16)


def _pre_call(x, g, w_in, cs, sgug, sguw, sgub, wc):
    b, seq, _ = x.shape
    tok = _tiles(seq)[0]
    grid = (b, seq // tok)
    const2 = lambda bi, i: (0, 0)
    const3 = lambda bi, i: (0, 0, 0)
    tile = lambda bi, i: (bi, i, 0)
    head_tile = lambda bi, i: (bi, 0, i, 0)
    head_tile_t = lambda bi, i: (bi, 0, 0, i)
    single = pl.Buffered(1)
    in_specs = [
        pl.BlockSpec((1, tok, D_MODEL), tile),
        pl.BlockSpec((1, D_MODEL), const2),
        pl.BlockSpec((D_MODEL, IN_COLS), const2, pipeline_mode=single),
        pl.BlockSpec((tok, 2 * HEAD_W), lambda bi, i: (i, 0)),
        pl.BlockSpec((1, SGU_W), const2),
        pl.BlockSpec((SGU_GROUPS, CHUNK, CHUNK), const3),
        pl.BlockSpec((CHUNK, SGU_W), const2),
        pl.BlockSpec((SGU_W, D_MODEL), const2),
    ]
    out_shape = [
        jax.ShapeDtypeStruct((b, seq, CONV_W), BF16),
        jax.ShapeDtypeStruct((b, seq, CONV_W), BF16),
        jax.ShapeDtypeStruct((b, N_HEADS, HEAD_W, seq), BF16),
        jax.ShapeDtypeStruct((b, N_HEADS, seq, HEAD_W), BF16),
        jax.ShapeDtypeStruct((b, N_HEADS, seq // KEY_BLOCK, HEAD_W, KEY_BLOCK), BF16),
        jax.ShapeDtypeStruct((b, seq, ATTN_W), BF16),
        jax.ShapeDtypeStruct((b, seq, D_MODEL), BF16),
        jax.ShapeDtypeStruct((b, seq, 2 * D_MODEL), BF16),
    ]
    out_specs = [
        pl.BlockSpec((1, tok, CONV_W), tile),
        pl.BlockSpec((1, tok, CONV_W), tile),
        pl.BlockSpec((1, N_HEADS, HEAD_W, tok), head_tile_t),
        pl.BlockSpec((1, N_HEADS, tok, HEAD_W), head_tile),
        pl.BlockSpec((1, N_HEADS, tok // KEY_BLOCK, HEAD_W, KEY_BLOCK), lambda bi, i: (bi, 0, i, 0, 0)),
        pl.BlockSpec((1, tok, ATTN_W), tile),
        pl.BlockSpec((1, tok, D_MODEL), tile),
        pl.BlockSpec((1, tok, 2 * D_MODEL), tile),
    ]
    return pl.pallas_call(
        _pre_kernel,
        grid=grid,
        in_specs=in_specs,
        out_specs=out_specs,
        out_shape=out_shape,
        scratch_shapes=[pltpu.VMEM((tok, SGU_W), F32)],
        compiler_params=pltpu.CompilerParams(
            dimension_semantics=("arbitrary", "arbitrary"),
            vmem_limit_bytes=VMEM_LIMIT_BYTES),
        name="pre",
    )(x, g, w_in, cs, sgug, sguw, sgub, wc)


def _attn_kernel(lam_ref, g_ref, qt_ref, k_ref, vt_ref, o_ref,
                 q2_ref, mref_ref, kmax_ref, p0_ref, p1_ref, acc_ref, l_ref, *, lam_init, tqi, tk):
    heads, tq = qt_ref.shape[1], qt_ref.shape[3]
    seq = k_ref.shape[2]
    nq = tq // tqi
    n = seq // tk
    nb = heads * nq
    items = nb * n
    r = 2 * tqi
    assert n & (n - 1) == 0 and nq & (nq - 1) == 0 and items >= 2
    log2_n = n.bit_length() - 1
    log2_nq = nq.bit_length() - 1
    lam = (jnp.exp(jnp.sum(lam_ref[0:1, :] * lam_ref[1:2, :], axis=-1, keepdims=True))
           - jnp.exp(jnp.sum(lam_ref[2:3, :] * lam_ref[3:4, :], axis=-1, keepdims=True))
           + lam_init)

    feat = lax.broadcasted_iota(jnp.int32, (HEAD_W, 1), 0)
    is_map0 = (feat % HEAD_DIM) < (HEAD_DIM // 2)

    @pl.when(pl.program_id(2) == 0)
    def _():
        lane_map0 = lax.broadcasted_iota(jnp.int32, (1, HEAD_W), 1) < HEAD_DIM
        sel = jnp.where(is_map0 == lane_map0, 1.0, 0.0).astype(BF16)
        col_map0 = lax.broadcasted_iota(jnp.int32, (1, r), 1) < tqi

        for hd in range(heads):
            def chunk_max(kc, best):
                k0 = pl.multiple_of(kc * tk, tk)
                kf = k_ref[0, hd, pl.ds(k0, tk), :].astype(F32)
                return jnp.maximum(best, jnp.max(_dot((kf * kf).astype(BF16), sel),
                                                 axis=0, keepdims=True))

            best = lax.fori_loop(0, n, chunk_max, jnp.zeros((1, HEAD_W), F32))
            best0 = jnp.max(jnp.where(lane_map0, best, 0.0), axis=1, keepdims=True)
            best1 = jnp.max(jnp.where(lane_map0, 0.0, best), axis=1, keepdims=True)
            kmax_ref[hd] = jnp.where(col_map0, best0, best1)

    zero = jnp.zeros((), BF16)
    for qb in range(nb):
        hd, j = divmod(qb, nq)
        qt = qt_ref[0, hd, :, j * tqi:(j + 1) * tqi]
        q2 = jnp.concatenate([jnp.where(is_map0, qt, zero), jnp.where(is_map0, zero, qt)], axis=1)
        q2_ref[qb] = q2
        q2f = q2.astype(F32)
        qn2 = jnp.sum(q2f * q2f, axis=0, keepdims=True)
        mref_ref[qb] = jnp.sqrt(qn2 * kmax_ref[hd]) * BOUND_SLACK

    def split(t):
        return lax.shift_right_logical(t, log2_n), lax.bitwise_and(t, n - 1)

    def k_chunk(qb, kc):
        hd = lax.shift_right_logical(qb, log2_nq)
        return k_ref[0, hd, pl.ds(pl.multiple_of(kc * tk, tk), tk), :]

    def vt_chunk(qb, kc):
        hd = lax.shift_right_logical(qb, log2_nq)
        kb0 = pl.multiple_of(kc * (tk // KEY_BLOCK), tk // KEY_BLOCK)
        return jnp.concatenate([vt_ref[0, hd, kb0 + i] for i in range(tk // KEY_BLOCK)], axis=1)

    def stage_qs(t, l, p_ref):
        qi, kc = split(t)
        p = jnp.exp2(_dot(k_chunk(qi, kc), q2_ref[qi]) - mref_ref[qi])
        p_ref[...] = p.astype(BF16)
        l = jnp.where(kc == 0, 0.0, l) + jnp.sum(p, axis=0, keepdims=True)
        return l

    def stage_p(t, l, acc, p_ref):
        qi, kc = split(t)
        acc = jnp.where(kc == 0, 0.0, acc) + _dot(vt_chunk(qi, kc), p_ref[...])
        acc_ref[qi] = acc
        l_ref[qi] = l
        return acc

    p_refs = (p0_ref, p1_ref)
    l0 = stage_qs(0, jnp.zeros((1, r), F32), p0_ref)

    def step(tau, c, carry):
        l_prev, acc = carry
        acc = stage_p(tau, l_prev, acc, p_refs[c])
        return stage_qs(tau + 1, l_prev, p_refs[1 - c]), acc

    def body(u, carry):
        for j in range(BODY_STEPS):
            carry = step(u * BODY_STEPS + j, j % 2, carry)
        return carry

    assert BODY_STEPS % 2 == 0
    trips = (items - 1) // BODY_STEPS
    carry = lax.fori_loop(0, trips, body, (l0, jnp.zeros((HEAD_W, r), F32)))
    for tau in range(trips * BODY_STEPS, items - 1):
        carry = step(tau, tau % 2, carry)
    stage_p(items - 1, *carry, p_refs[(items - 1) % 2])

    @pl.when(jnp.logical_not(jnp.min(l_ref[...]) >= MIN_DENOMINATOR))
    def _():
        def exact_block(j, carry):
            def chunk(kc, mla):
                m, l, acc = mla
                s = _dot(k_chunk(j, kc), q2_ref[j])
                m_new = jnp.maximum(m, jnp.max(s, axis=0, keepdims=True))
                alpha = jnp.exp2(m - m_new)
                p = jnp.exp2(s - m_new)
                l = alpha * l + jnp.sum(p, axis=0, keepdims=True)
                acc = alpha * acc + _dot(vt_chunk(j, kc), p.astype(BF16))
                return m_new, l, acc

            init = (jnp.full((1, r), -jnp.inf, F32), jnp.zeros((1, r), F32),
                    jnp.zeros((HEAD_W, r), F32))
            _, l, acc = lax.fori_loop(0, n, chunk, init)
            acc_ref[j] = acc
            l_ref[j] = l
            return carry

        lax.fori_loop(0, nb, exact_block, 0)

    gain = g_ref[...] * (1.0 - lam_init)
    for qb in range(nb):
        hd, j = divmod(qb, nq)
        a = acc_ref[qb]
        lj = l_ref[qb]
        o = a[:, :tqi] / lj[:, :tqi] - lam * (a[:, tqi:] / lj[:, tqi:])
        o = o * lax.rsqrt(jnp.mean(o * o, axis=0, keepdims=True) + RMS_EPS) * gain
        o_ref[0, j * tqi:(j + 1) * tqi, hd * HEAD_W:(hd + 1) * HEAD_W] = o.T.astype(o_ref.dtype)


def _attn_call(lam_vecs, subln_g, qt, k, vt, lam_init):
    b, _, seq, _ = k.shape
    _, tq, tqi, tk, heads = _tiles(seq)
    nb = heads * (tq // tqi)
    r = 2 * tqi
    grid = (b, N_HEADS // heads, seq // tq)
    const2 = lambda bi, hd, i: (0, 0)
    return pl.pallas_call(
        functools.partial(_attn_kernel, lam_init=lam_init, tqi=tqi, tk=tk),
        grid=grid,
        in_specs=[
            pl.BlockSpec((4, HEAD_DIM), const2),
            pl.BlockSpec((HEAD_W, 1), const2),
            pl.BlockSpec((1, heads, HEAD_W, tq), lambda bi, hd, i: (bi, hd, 0, i)),
            pl.BlockSpec((1, heads, seq, HEAD_W), lambda bi, hd, i: (bi, hd, 0, 0)),
            pl.BlockSpec((1, heads, seq // KEY_BLOCK, HEAD_W, KEY_BLOCK),
                         lambda bi, hd, i: (bi, hd, 0, 0, 0)),
        ],
        out_specs=pl.BlockSpec((1, tq, heads * HEAD_W), lambda bi, hd, i: (bi, i, hd)),
        out_shape=jax.ShapeDtypeStruct((b, seq, ATTN_W), BF16),
        scratch_shapes=[
            pltpu.VMEM((nb, HEAD_W, r), BF16),
            pltpu.VMEM((nb, 1, r), F32),
            pltpu.VMEM((heads, 1, r), F32),
            pltpu.VMEM((tk, r), BF16), pltpu.VMEM((tk, r), BF16),
            pltpu.VMEM((nb, HEAD_W, r), F32),
            pltpu.VMEM((nb, 1, r), F32),
        ],
        compiler_params=pltpu.CompilerParams(
            dimension_semantics=("arbitrary", "arbitrary", "arbitrary"),
            vmem_limit_bytes=VMEM_LIMIT_BYTES),
        name="attn",
    )(lam_vecs, subln_g, qt, k, vt)


def _post_kernel(x_ref, aprev_ref, acur_ref, anext_ref, ga_ref, o_ref, gb_ref, pc_ref, m01_ref,
                 cw_ref, cb_ref, cg_ref, wa_ref, wb_ref, wo_ref, fg_ref,
                 y_ref, abuf_ref, conv_ref, *, final, rows):
    tok = x_ref.shape[1]
    i = pl.program_id(1)
    last = pl.num_programs(1) - 1
    prev = aprev_ref[0].astype(F32)
    nxt = anext_ref[0].astype(F32)
    abuf_ref[0:CONV_HALO, :] = jnp.where(i > 0, prev, 0.0)
    abuf_ref[CONV_HALO:CONV_HALO + tok, :] = acur_ref[0].astype(F32)
    abuf_ref[CONV_HALO + tok:, :] = jnp.where(i < last, nxt, 0.0)

    shift = CONV_HALO - CONV_K // 2
    win_rows = rows + 2 * CONV_HALO

    def conv_rows(r, carry):
        r0 = pl.multiple_of(r * rows, rows)
        win = abuf_ref[pl.ds(r0, win_rows), :]
        acc = jnp.zeros((rows, CONV_W), F32)
        for s in range(SUBLANES):
            ws = win if s == 0 else pltpu.roll(win, win_rows - s, 0)
            for a in range(win_rows // SUBLANES):
                k = a * SUBLANES + s - shift
                if 0 <= k < CONV_K:
                    acc = acc + cw_ref[k:k + 1, :] * ws[a * SUBLANES:a * SUBLANES + rows]
        conv_ref[pl.ds(r0, rows), :] = acc
        return carry

    lax.fori_loop(0, tok // rows, conv_rows, 0)

    a = jax.nn.silu(_rms(conv_ref[...] + cb_ref[...], cg_ref[...]))
    ya = _dot((a * ga_ref[0].astype(F32)).astype(BF16), wa_ref[...])
    yb = _dot((o_ref[0].astype(F32) * gb_ref[0].astype(F32)).astype(BF16), wb_ref[...])
    y = (m01_ref[0, :, :D_MODEL].astype(F32) * ya + m01_ref[0, :, D_MODEL:].astype(F32) * yb
         + pc_ref[0].astype(F32))
    out = x_ref[0] + _dot(y.astype(BF16), wo_ref[...])
    if final:
        out = _rms(out, fg_ref[...])
    y_ref[0] = out


def _post_call(x, aglu, ga, o, gb, pc, m01, cw, cb, cg, wa, wb, wo, fg, final):
    b, seq, _ = x.shape
    tok = _tiles(seq)[0]
    halo_blocks = tok // CONV_HALO
    n_halo = seq // CONV_HALO
    grid = (b, seq // tok)
    tile = lambda bi, i: (bi, i, 0)
    const2 = lambda bi, i: (0, 0)
    in_specs = [
        pl.BlockSpec((1, tok, D_MODEL), tile),
        pl.BlockSpec((1, CONV_HALO, CONV_W),
                     lambda bi, i: (bi, jnp.maximum(i * halo_blocks - 1, 0), 0)),
        pl.BlockSpec((1, tok, CONV_W), tile),
        pl.BlockSpec((1, CONV_HALO, CONV_W),
                     lambda bi, i: (bi, jnp.minimum((i + 1) * halo_blocks, n_halo - 1), 0)),
        pl.BlockSpec((1, tok, CONV_W), tile),
        pl.BlockSpec((1, tok, ATTN_W), tile),
        pl.BlockSpec((1, tok, ATTN_W), tile),
        pl.BlockSpec((1, tok, D_MODEL), tile),
        pl.BlockSpec((1, tok, 2 * D_MODEL), tile),
        pl.BlockSpec((CONV_K, CONV_W), const2),
        pl.BlockSpec((1, CONV_W), const2),
        pl.BlockSpec((1, CONV_W), const2),
        pl.BlockSpec((CONV_W, D_MODEL), const2),
        pl.BlockSpec((ATTN_W, D_MODEL), const2),
        pl.BlockSpec((D_MODEL, D_MODEL), const2),
        pl.BlockSpec((1, D_MODEL), const2),
    ]
    return pl.pallas_call(
        functools.partial(_post_kernel, final=final, rows=32),
        grid=grid,
        in_specs=in_specs,
        out_specs=pl.BlockSpec((1, tok, D_MODEL), tile),
        out_shape=jax.ShapeDtypeStruct((b, seq, D_MODEL), F32),
        scratch_shapes=[pltpu.VMEM((tok + 2 * CONV_HALO, CONV_W), F32),
                        pltpu.VMEM((tok, CONV_W), F32)],
        compiler_params=pltpu.CompilerParams(
            dimension_semantics=("arbitrary", "arbitrary"),
            vmem_limit_bytes=VMEM_LIMIT_BYTES),
        name="post",
    )(x, aglu, aglu, aglu, ga, o, gb, pc, m01, cw, cb, cg, wa, wb, wo, fg)


def _permute_heads(w):
    k = w.shape[0]
    w = w.reshape(k, N_HEADS, 2, 2, HEAD_DIM // 2)
    return w.transpose(0, 1, 3, 2, 4).reshape(k, ATTN_W)


def _rotary_table(seq):
    half = HEAD_DIM // 2
    inv = ROPE_THETA ** (-jnp.arange(half, dtype=F32) / half)
    ang = jnp.arange(seq, dtype=jnp.int32).astype(F32)[:, None] * inv[None, :]
    cos = jnp.tile(jnp.cos(ang), (1, 4))
    sin = jnp.tile(jnp.sin(ang), (1, 4))
    sign = jnp.where(jnp.arange(HEAD_W) < HEAD_DIM, -1.0, 1.0).astype(F32)
    return jnp.concatenate([cos, sin * sign[None, :]], axis=1)


def kernel(x_prompt, x_sample, norm_g, w_in, conv_w, conv_b, conv_norm_g, w_proj_a,
           lam_q1, lam_k1, lam_q2, lam_k2, subln_g, w_proj_b,
           sgu_norm_g, sgu_w, sgu_b, w_proj_c, w_out, final_g):
    w_in_p = jnp.concatenate(
        [w_in[:, :, :O_Q], jax.vmap(_permute_heads)(w_in[:, :, O_Q:O_K]),
         jax.vmap(_permute_heads)(w_in[:, :, O_K:O_V]), w_in[:, :, O_V:]], axis=-1).astype(BF16)
    sgu_w_b = sgu_w.astype(BF16)
    sgu_bias = jnp.repeat(jnp.swapaxes(sgu_b, 1, 2), SGU_GROUP_W, axis=2)
    wc_b = w_proj_c.astype(BF16)
    wa_b = w_proj_a.astype(BF16)
    wb_b = w_proj_b.astype(BF16)
    wo_b = w_out.astype(BF16)
    lam_vecs = jnp.stack([lam_q1, lam_k1, lam_q2, lam_k2], axis=1).astype(F32)
    cs = _rotary_table(max(x_prompt.shape[1], x_sample.shape[1]))

    outs = []
    for x in (x_prompt, x_sample):
        for l in range(DEPTH):
            aglu, ga, qt, k, vt, gb, pc, m01 = _pre_call(
                x, norm_g[l][None], w_in_p[l], cs, sgu_norm_g[l][None], sgu_w_b[l], sgu_bias[l],
                wc_b[l])
            o = _attn_call(lam_vecs[l], subln_g[l][:, None], qt, k, vt, _lambda_init(l))
            x = _post_call(x, aglu, ga, o, gb, pc, m01, conv_w[l], conv_b[l][None],
                           conv_norm_g[l][None], wa_b[l], wb_b[l], wo_b[l], final_g[None],
                           final=(l == DEPTH - 1))
        outs.append(x)
    return tuple(outs)
```

```python
import functools
import math

import jax
import jax.numpy as jnp
from jax import lax
from jax.experimental import pallas as pl
from jax.experimental.pallas import tpu as pltpu

D_MODEL = 1024
DEPTH = 2
CONV_W = 512
CONV_K = 31
CONV_HALO = 16
SUBLANES = 8
LANES = 128
N_HEADS = 8
HEAD_DIM = 64
HEAD_W = 2 * HEAD_DIM
ATTN_W = N_HEADS * HEAD_W
ROPE_THETA = 10000.0
SGU_W = 512
SGU_GROUPS = 4
SGU_GROUP_W = SGU_W // SGU_GROUPS
CHUNK = 128
N_BRANCH = 3
RMS_EPS = 1e-6
KEY_BLOCK = 128
BODY_STEPS = 2
BOUND_SLACK = 1.01
MIN_DENOMINATOR = 2.0 ** -64

_COLS = (2 * CONV_W, CONV_W, ATTN_W, ATTN_W, ATTN_W, ATTN_W, SGU_W, SGU_W, SGU_W, N_BRANCH * D_MODEL)
_OFF = [0]
for _c in _COLS:
    _OFF.append(_OFF[-1] + _c)
(O_AIN, O_AGATE, O_Q, O_K, O_V, O_BGATE, O_CU, O_CV, O_CGATE, O_MERGE, IN_COLS) = _OFF

F32 = jnp.float32
BF16 = jnp.bfloat16

VMEM_LIMIT_BYTES = 56 * 1024 * 1024


def _lambda_init(layer_idx):
    return 0.8 - 0.6 * math.exp(-0.3 * layer_idx)


def _rms(x, g):
    return x * lax.rsqrt(jnp.mean(x * x, axis=-1, keepdims=True) + RMS_EPS) * g


def _dot(a, b):
    return jnp.dot(a, b, preferred_element_type=F32)


def _tiles(seq):
    tok = 256
    tq = min(seq, 4096)
    tqi = 256 if seq <= 2048 else 128
    tk = min(seq, 8192)
    heads = 8 if seq <= 2048 else 1
    return tok, tq, tqi, tk, heads


def _pre_kernel(x_ref, xprev_ref, xnext_ref, g_ref, w_ref, cs_ref, sgug_ref, sguw_ref, sgub_ref,
                wc_ref, cw_ref, cb_ref, cg_ref, wa_ref,
                qt_ref, k_ref, vt_ref, gb_ref, pc_ref, m1_ref,
                mix_ref, abuf_ref, conv_ref, *, rows):
    tok = x_ref.shape[1]
    i = pl.program_id(1)
    last = pl.num_programs(1) - 1
    norm = lambda ref: _rms(ref[0], g_ref[...]).astype(BF16)
    h_ext = jnp.concatenate([norm(xprev_ref), norm(x_ref), norm(xnext_ref)], axis=0)
    h = h_ext[CONV_HALO:CONV_HALO + tok]

    def proj(c0, c1):
        return _dot(h, w_ref[:, c0:c1])

    za = _dot(h_ext, w_ref[:, O_AIN:O_AGATE])
    glu = za[:, :CONV_W] * jax.nn.sigmoid(za[:, CONV_W:])
    abuf_ref[0:CONV_HALO, :] = jnp.where(i > 0, glu[0:CONV_HALO], 0.0)
    abuf_ref[CONV_HALO:CONV_HALO + tok, :] = glu[CONV_HALO:CONV_HALO + tok]
    abuf_ref[CONV_HALO + tok:, :] = jnp.where(i < last, glu[CONV_HALO + tok:], 0.0)

    shift = CONV_HALO - CONV_K // 2
    win_rows = rows + 2 * CONV_HALO

    def conv_block(r0, c0):
        win = abuf_ref[r0:r0 + win_rows, c0:c0 + LANES]
        acc = jnp.zeros((rows // SUBLANES, SUBLANES, LANES), F32)
        for s in range(SUBLANES):
            ws = win if s == 0 else pltpu.roll(win, win_rows - s, 0)
            for a in range(win_rows // SUBLANES):
                k = a * SUBLANES + s - shift
                if 0 <= k < CONV_K:
                    tap = ws[a * SUBLANES:a * SUBLANES + rows].reshape(acc.shape)
                    acc = acc + cw_ref[k, :, c0:c0 + LANES] * tap
        conv_ref[r0:r0 + rows, c0:c0 + LANES] = acc.reshape(rows, LANES)

    pending = [(r0, c0) for r0 in range(0, tok, rows) for c0 in range(0, CONV_W, LANES)]
    share = -(-len(pending) // 8)

    def conv_some(count=share):
        for _ in range(min(count, len(pending))):
            conv_block(*pending.pop(0))

    cos = cs_ref[:, :HEAD_W]
    sin = cs_ref[:, HEAD_W:]
    lane = lax.broadcasted_iota(jnp.int32, (1, HEAD_W), 1)
    first_half = (lane % HEAD_DIM) < (HEAD_DIM // 2)

    def rotary(s):
        partner = jnp.where(first_half, pltpu.roll(s, HEAD_W - HEAD_DIM // 2, 1),
                            pltpu.roll(s, HEAD_DIM // 2, 1))
        return s * cos + partner * sin

    scale = math.log2(math.e) / math.sqrt(HEAD_DIM)
    zq = proj(O_Q, O_K)
    for hd in range(N_HEADS):
        s = zq[:, hd * HEAD_W:(hd + 1) * HEAD_W]
        qt_ref[0, hd] = (rotary(s) * scale).T.astype(BF16)
    conv_some()
    zk = proj(O_K, O_V)
    for hd in range(N_HEADS):
        s = zk[:, hd * HEAD_W:(hd + 1) * HEAD_W]
        k_ref[0, hd] = rotary(s).astype(BF16)
    conv_some()
    zv = proj(O_V, O_BGATE)
    for hd in range(N_HEADS):
        for c in range(tok // KEY_BLOCK):
            vt_ref[0, hd, c] = zv[c * KEY_BLOCK:(c + 1) * KEY_BLOCK,
                                  hd * HEAD_W:(hd + 1) * HEAD_W].T.astype(BF16)
    conv_some()
    gb_ref[0] = jax.nn.silu(proj(O_BGATE, O_CU)).astype(BF16)
    conv_some()

    cv = _rms(proj(O_CV, O_CGATE), sgug_ref[...]).astype(BF16)
    for n in range(tok // CHUNK):
        for g in range(SGU_GROUPS):
            blk = cv[n * CHUNK:(n + 1) * CHUNK, g * SGU_GROUP_W:(g + 1) * SGU_GROUP_W]
            mix_ref[n * CHUNK:(n + 1) * CHUNK, g * SGU_GROUP_W:(g + 1) * SGU_GROUP_W] = (
                _dot(sguw_ref[g], blk) + sgub_ref[:, g * SGU_GROUP_W:(g + 1) * SGU_GROUP_W])
    conv_some()
    u = proj(O_CU, O_CV) * mix_ref[...] * jax.nn.silu(proj(O_CGATE, O_MERGE))
    yc = _dot(u.astype(BF16), wc_ref[...])
    conv_some()

    m1_ref[0] = jax.nn.sigmoid(proj(O_MERGE + D_MODEL, O_MERGE + 2 * D_MODEL)).astype(BF16)
    conv_some()
    gated_c = jax.nn.sigmoid(proj(O_MERGE + 2 * D_MODEL, IN_COLS)) * yc
    conv_some(len(pending))
    a = jax.nn.silu(_rms(conv_ref[...] + cb_ref[...], cg_ref[...]))
    ya = _dot((a * jax.nn.silu(proj(O_AGATE, O_Q))).astype(BF16), wa_ref[...])
    pc_ref[0] = (jax.nn.sigmoid(proj(O_MERGE, O_MERGE + D_MODEL)) * ya + gated_c).astype(BF16)


def _pre_call(layer, x, g, w_in, cs, sgug, sguw, sgub, wc, cw, cb, cg, wa):
    b, seq, _ = x.shape
    tok = _tiles(seq)[0]
    halo_blocks = tok // CONV_HALO
    n_halo = seq // CONV_HALO
    grid = (b, seq // tok)
    lyr3 = lambda bi, i: (layer, 0, 0)
    tile = lambda bi, i: (bi, i, 0)
    head_tile = lambda bi, i: (bi, 0, i, 0)
    head_tile_t = lambda bi, i: (bi, 0, 0, i)
    single = pl.Buffered(1)
    in_specs = [
        pl.BlockSpec((1, tok, D_MODEL), tile),
        pl.BlockSpec((1, CONV_HALO, D_MODEL),
                     lambda bi, i: (bi, jnp.maximum(i * halo_blocks - 1, 0), 0)),
        pl.BlockSpec((1, CONV_HALO, D_MODEL),
                     lambda bi, i: (bi, jnp.minimum((i + 1) * halo_blocks, n_halo - 1), 0)),
        pl.BlockSpec((None, 1, D_MODEL), lyr3),
        pl.BlockSpec((None, D_MODEL, IN_COLS), lyr3, pipeline_mode=single),
        pl.BlockSpec((tok, 2 * HEAD_W), lambda bi, i: (i, 0)),
        pl.BlockSpec((None, 1, SGU_W), lyr3),
        pl.BlockSpec((None, SGU_GROUPS, CHUNK, CHUNK), lambda bi, i: (layer, 0, 0, 0)),
        pl.BlockSpec((None, CHUNK, SGU_W), lyr3),
        pl.BlockSpec((None, SGU_W, D_MODEL), lyr3),
        pl.BlockSpec((None, CONV_K, SUBLANES, CONV_W), lambda bi, i: (layer, 0, 0, 0)),
        pl.BlockSpec((None, 1, CONV_W), lyr3),
        pl.BlockSpec((None, 1, CONV_W), lyr3),
        pl.BlockSpec((None, CONV_W, D_MODEL), lyr3),
    ]
    out_shape = [
        jax.ShapeDtypeStruct((b, N_HEADS, HEAD_W, seq), BF16),
        jax.ShapeDtypeStruct((b, N_HEADS, seq, HEAD_W), BF16),
        jax.ShapeDtypeStruct((b, N_HEADS, seq // KEY_BLOCK, HEAD_W, KEY_BLOCK), BF16),
        jax.ShapeDtypeStruct((b, seq, ATTN_W), BF16),
        jax.ShapeDtypeStruct((b, seq, D_MODEL), BF16),
        jax.ShapeDtypeStruct((b, seq, D_MODEL), BF16),
    ]
    out_specs = [
        pl.BlockSpec((1, N_HEADS, HEAD_W, tok), head_tile_t),
        pl.BlockSpec((1, N_HEADS, tok, HEAD_W), head_tile),
        pl.BlockSpec((1, N_HEADS, tok // KEY_BLOCK, HEAD_W, KEY_BLOCK), lambda bi, i: (bi, 0, i, 0, 0)),
        pl.BlockSpec((1, tok, ATTN_W), tile),
        pl.BlockSpec((1, tok, D_MODEL), tile),
        pl.BlockSpec((1, tok, D_MODEL), tile),
    ]
    return pl.pallas_call(
        functools.partial(_pre_kernel, rows=32),
        grid=grid,
        in_specs=in_specs,
        out_specs=out_specs,
        out_shape=out_shape,
        scratch_shapes=[pltpu.VMEM((tok, SGU_W), F32),
                        pltpu.VMEM((tok + 2 * CONV_HALO, CONV_W), F32),
                        pltpu.VMEM((tok, CONV_W), F32)],
        compiler_params=pltpu.CompilerParams(
            dimension_semantics=("arbitrary", "arbitrary"),
            vmem_limit_bytes=VMEM_LIMIT_BYTES),
        name="pre",
    )(x, x, x, g, w_in, cs, sgug, sguw, sgub, wc, cw, cb, cg, wa)


def _attn_kernel(lam_ref, g_ref, qt_ref, k_ref, vt_ref, o_ref,
                 q2_ref, mref_ref, kmax_ref, p0_ref, p1_ref, l_ref, *maybe_acc_ref,
                 lam_init, tqi, tk):
    heads, tq = qt_ref.shape[1], qt_ref.shape[3]
    seq = k_ref.shape[2]
    nq = tq // tqi
    n = seq // tk
    nb = heads * nq
    items = nb * n
    r = 2 * tqi
    assert n & (n - 1) == 0 and nq & (nq - 1) == 0 and items >= 2
    log2_n = n.bit_length() - 1
    log2_nq = nq.bit_length() - 1
    lam = (jnp.exp(jnp.sum(lam_ref[0:1, :] * lam_ref[1:2, :], axis=-1, keepdims=True))
           - jnp.exp(jnp.sum(lam_ref[2:3, :] * lam_ref[3:4, :], axis=-1, keepdims=True))
           + lam_init)

    feat = lax.broadcasted_iota(jnp.int32, (HEAD_W, 1), 0)
    is_map0 = feat < HEAD_DIM

    @pl.when(pl.program_id(2) == 0)
    def _():
        lane_map0 = lax.broadcasted_iota(jnp.int32, (1, HEAD_W), 1) < HEAD_DIM
        sel = jnp.where(is_map0 == lane_map0, 1.0, 0.0).astype(BF16)
        col_map0 = lax.broadcasted_iota(jnp.int32, (1, r), 1) < tqi

        for hd in range(heads):
            def chunk_max(kc, best):
                k0 = pl.multiple_of(kc * tk, tk)
                kf = k_ref[0, hd, pl.ds(k0, tk), :].astype(F32)
                return jnp.maximum(best, jnp.max(_dot((kf * kf).astype(BF16), sel),
                                                 axis=0, keepdims=True))

            best = lax.fori_loop(0, n, chunk_max, jnp.zeros((1, HEAD_W), F32))
            best0 = jnp.max(jnp.where(lane_map0, best, 0.0), axis=1, keepdims=True)
            best1 = jnp.max(jnp.where(lane_map0, 0.0, best), axis=1, keepdims=True)
            kmax_ref[hd] = jnp.where(col_map0, best0, best1)

    zero = jnp.zeros((), BF16)
    for qb in range(nb):
        hd, j = divmod(qb, nq)
        qt = qt_ref[0, hd, :, j * tqi:(j + 1) * tqi]
        q2 = jnp.concatenate([jnp.where(is_map0, qt, zero), jnp.where(is_map0, zero, qt)], axis=1)
        q2_ref[qb] = q2
        q2f = q2.astype(F32)
        qn2 = jnp.sum(q2f * q2f, axis=0, keepdims=True)
        mref_ref[qb] = jnp.sqrt(qn2 * kmax_ref[hd]) * BOUND_SLACK

    def split(t):
        return lax.shift_right_logical(t, log2_n), lax.bitwise_and(t, n - 1)

    def k_chunk(qb, kc):
        hd = lax.shift_right_logical(qb, log2_nq)
        return k_ref[0, hd, pl.ds(pl.multiple_of(kc * tk, tk), tk), :]

    def vt_chunk(qb, kc):
        hd = lax.shift_right_logical(qb, log2_nq)
        kb0 = pl.multiple_of(kc * (tk // KEY_BLOCK), tk // KEY_BLOCK)
        return jnp.concatenate([vt_ref[0, hd, kb0 + i] for i in range(tk // KEY_BLOCK)], axis=1)

    def stage_qs(t, l, p_ref):
        qi, kc = split(t)
        p = jnp.exp2(_dot(k_chunk(qi, kc), q2_ref[qi]) - mref_ref[qi])
        p_ref[...] = p.astype(BF16)
        l = jnp.where(kc == 0, 0.0, l) + jnp.sum(p, axis=0, keepdims=True)
        return l

    gain = g_ref[...] * (1.0 - lam_init)

    def store_output(qb, acc, l):
        o = acc[:, :tqi] / l[:, :tqi] - lam * (acc[:, tqi:] / l[:, tqi:])
        o = o * lax.rsqrt(jnp.mean(o * o, axis=0, keepdims=True) + RMS_EPS) * gain
        if isinstance(qb, int):
            hd, q0 = qb // nq, (qb % nq) * tqi
        else:
            hd = lax.shift_right_logical(qb, log2_nq)
            q0 = pl.multiple_of(lax.bitwise_and(qb, nq - 1) * tqi, tqi)
        o_ref[0, hd, pl.ds(q0, tqi), :] = o.T.astype(o_ref.dtype)

    def stage_p(t, l, acc, p_ref):
        qi, kc = split(t)
        acc = jnp.where(kc == 0, 0.0, acc) + _dot(vt_chunk(qi, kc), p_ref[...])
        l_ref[qi] = l
        if maybe_acc_ref:
            maybe_acc_ref[0][qi] = acc
        else:
            store_output(qi, acc, l)
        return acc

    p_refs = (p0_ref, p1_ref)
    l0 = stage_qs(0, jnp.zeros((1, r), F32), p0_ref)

    def step(tau, c, carry):
        l_prev, acc = carry
        acc = stage_p(tau, l_prev, acc, p_refs[c])
        return stage_qs(tau + 1, l_prev, p_refs[1 - c]), acc

    def body(u, carry):
        for j in range(BODY_STEPS):
            carry = step(u * BODY_STEPS + j, j % 2, carry)
        return carry

    assert BODY_STEPS % 2 == 0
    trips = (items - 1) // BODY_STEPS
    carry = lax.fori_loop(0, trips, body, (l0, jnp.zeros((HEAD_W, r), F32)))
    for tau in range(trips * BODY_STEPS, items - 1):
        carry = step(tau, tau % 2, carry)
    stage_p(items - 1, *carry, p_refs[(items - 1) % 2])

    if maybe_acc_ref:
        for qb in range(nb):
            store_output(qb, maybe_acc_ref[0][qb], l_ref[qb])

    @pl.when(jnp.logical_not(jnp.min(l_ref[...]) >= MIN_DENOMINATOR))
    def _():
        def exact_block(j, carry):
            def chunk(kc, mla):
                m, l, acc = mla
                s = _dot(k_chunk(j, kc), q2_ref[j])
                m_new = jnp.maximum(m, jnp.max(s, axis=0, keepdims=True))
                alpha = jnp.exp2(m - m_new)
                p = jnp.exp2(s - m_new)
                l = alpha * l + jnp.sum(p, axis=0, keepdims=True)
                acc = alpha * acc + _dot(vt_chunk(j, kc), p.astype(BF16))
                return m_new, l, acc

            init = (jnp.full((1, r), -jnp.inf, F32), jnp.zeros((1, r), F32),
                    jnp.zeros((HEAD_W, r), F32))
            _, l, acc = lax.fori_loop(0, n, chunk, init)
            store_output(j, acc, l)
            return carry

        lax.fori_loop(0, nb, exact_block, 0)


def _attn_call(layer, lam_vecs, subln_g, qt, k, vt, lam_init):
    b, _, seq, _ = k.shape
    _, tq, tqi, tk, heads = _tiles(seq)
    nb = heads * (tq // tqi)
    r = 2 * tqi
    grid = (b, N_HEADS // heads, seq // tq)
    lyr3 = lambda bi, hd, i: (layer, 0, 0)
    return pl.pallas_call(
        functools.partial(_attn_kernel, lam_init=lam_init, tqi=tqi, tk=tk),
        grid=grid,
        in_specs=[
            pl.BlockSpec((None, 4, HEAD_DIM), lyr3),
            pl.BlockSpec((None, HEAD_W, 1), lyr3),
            pl.BlockSpec((1, heads, HEAD_W, tq), lambda bi, hd, i: (bi, hd, 0, i)),
            pl.BlockSpec((1, heads, seq, HEAD_W), lambda bi, hd, i: (bi, hd, 0, 0)),
            pl.BlockSpec((1, heads, seq // KEY_BLOCK, HEAD_W, KEY_BLOCK),
                         lambda bi, hd, i: (bi, hd, 0, 0, 0)),
        ],
        out_specs=pl.BlockSpec((1, heads, tq, HEAD_W), lambda bi, hd, i: (bi, hd, i, 0)),
        out_shape=jax.ShapeDtypeStruct((b, N_HEADS, seq, HEAD_W), BF16),
        scratch_shapes=[
            pltpu.VMEM((nb, HEAD_W, r), BF16),
            pltpu.VMEM((nb, 1, r), F32),
            pltpu.VMEM((heads, 1, r), F32),
            pltpu.VMEM((tk, r), BF16), pltpu.VMEM((tk, r), BF16),
            pltpu.VMEM((nb, 1, r), F32),
            *([pltpu.VMEM((nb, HEAD_W, r), F32)] if seq > tk else []),
        ],
        compiler_params=pltpu.CompilerParams(
            dimension_semantics=("arbitrary", "arbitrary", "arbitrary"),
            vmem_limit_bytes=VMEM_LIMIT_BYTES),
        name="attn",
    )(lam_vecs, subln_g, qt, k, vt)


def _post_kernel(x_ref, o_ref, gb_ref, pc_ref, m1_ref, wb_ref, wo_ref, fg_ref, y_ref, *, final):
    o = jnp.concatenate([o_ref[0, hd] for hd in range(N_HEADS)], axis=1)
    yb = _dot((o.astype(F32) * gb_ref[0].astype(F32)).astype(BF16), wb_ref[...])
    y = m1_ref[0].astype(F32) * yb + pc_ref[0].astype(F32)
    out = x_ref[0] + _dot(y.astype(BF16), wo_ref[...])
    if final:
        out = _rms(out, fg_ref[...])
    y_ref[0] = out


def _post_call(layer, x, o, gb, pc, m1, wb, wo, fg, final):
    b, seq, _ = x.shape
    tok = _tiles(seq)[0]
    grid = (b, seq // tok)
    tile = lambda bi, i: (bi, i, 0)
    lyr3 = lambda bi, i: (layer, 0, 0)
    in_specs = [
        pl.BlockSpec((1, tok, D_MODEL), tile),
        pl.BlockSpec((1, N_HEADS, tok, HEAD_W), lambda bi, i: (bi, 0, i, 0)),
        pl.BlockSpec((1, tok, ATTN_W), tile),
        pl.BlockSpec((1, tok, D_MODEL), tile),
        pl.BlockSpec((1, tok, D_MODEL), tile),
        pl.BlockSpec((None, ATTN_W, D_MODEL), lyr3),
        pl.BlockSpec((None, D_MODEL, D_MODEL), lyr3),
        pl.BlockSpec((1, D_MODEL), lambda bi, i: (0, 0)),
    ]
    return pl.pallas_call(
        functools.partial(_post_kernel, final=final),
        grid=grid,
        in_specs=in_specs,
        out_specs=pl.BlockSpec((1, tok, D_MODEL), tile),
        out_shape=jax.ShapeDtypeStruct((b, seq, D_MODEL), F32),
        compiler_params=pltpu.CompilerParams(
            dimension_semantics=("arbitrary", "arbitrary"),
            vmem_limit_bytes=VMEM_LIMIT_BYTES),
        name="post",
    )(x, o, gb, pc, m1, wb, wo, fg)


def _rotary_table(seq):
    half = HEAD_DIM // 2
    inv = ROPE_THETA ** (-jnp.arange(half, dtype=F32) / half)
    ang = jnp.arange(seq, dtype=jnp.int32).astype(F32)[:, None] * inv[None, :]
    cos = jnp.tile(jnp.cos(ang), (1, 4))
    sin = jnp.tile(jnp.sin(ang), (1, 4))
    sign = jnp.where(jnp.arange(HEAD_W) % HEAD_DIM < half, -1.0, 1.0).astype(F32)
    return jnp.concatenate([cos, sin * sign[None, :]], axis=1)


def kernel(x_prompt, x_sample, norm_g, w_in, conv_w, conv_b, conv_norm_g, w_proj_a,
           lam_q1, lam_k1, lam_q2, lam_k2, subln_g, w_proj_b,
           sgu_norm_g, sgu_w, sgu_b, w_proj_c, w_out, final_g):
    w_in_b = w_in.astype(BF16)
    sgu_w_b = sgu_w.astype(BF16)
    sgu_bias = jnp.repeat(jnp.swapaxes(sgu_b, 1, 2), SGU_GROUP_W, axis=2)
    wc_b = w_proj_c.astype(BF16)
    conv_w_rep = jnp.broadcast_to(conv_w[:, :, None, :], (DEPTH, CONV_K, SUBLANES, CONV_W))
    wa_b = w_proj_a.astype(BF16)
    wb_b = w_proj_b.astype(BF16)
    wo_b = w_out.astype(BF16)
    lam_vecs = jnp.stack([lam_q1, lam_k1, lam_q2, lam_k2], axis=1).astype(F32)
    cs = _rotary_table(max(x_prompt.shape[1], x_sample.shape[1]))

    row = lambda p: p[:, None, :]
    outs = []
    for x in (x_prompt, x_sample):
        for l in range(DEPTH):
            qt, k, vt, gb, pc, m1 = _pre_call(
                l, x, row(norm_g), w_in_b, cs, row(sgu_norm_g), sgu_w_b, sgu_bias, wc_b,
                conv_w_rep, row(conv_b), row(conv_norm_g), wa_b)
            o = _attn_call(l, lam_vecs, subln_g[:, :, None], qt, k, vt, _lambda_init(l))
            x = _post_call(l, x, o, gb, pc, m1, wb_b, wo_b, final_g[None],
                           final=(l == DEPTH - 1))
        outs.append(x)
    return tuple(outs)
```

```python
import functools
import math

import jax
import jax.numpy as jnp
from jax import lax
from jax.experimental import pallas as pl
from jax.experimental.pallas import tpu as pltpu

D_MODEL = 1024
DEPTH = 2
CONV_W = 512
CONV_K = 31
CONV_HALO = 16
SUBLANES = 8
LANES = 128
N_HEADS = 8
HEAD_DIM = 64
HEAD_W = 2 * HEAD_DIM
ATTN_W = N_HEADS * HEAD_W
ROPE_THETA = 10000.0
SGU_W = 512
SGU_GROUPS = 4
SGU_GROUP_W = SGU_W // SGU_GROUPS
CHUNK = 128
N_BRANCH = 3
RMS_EPS = 1e-6
POST_TOKENS = 512
KEY_BLOCK = 128
BOUND_SLACK = 1.01
MIN_DENOMINATOR = 2.0 ** -64

_COLS = (2 * CONV_W, CONV_W, ATTN_W, ATTN_W, ATTN_W, ATTN_W, SGU_W, SGU_W, SGU_W, N_BRANCH * D_MODEL)
_OFF = [0]
for _c in _COLS:
    _OFF.append(_OFF[-1] + _c)
(O_AIN, O_AGATE, O_Q, O_K, O_V, O_BGATE, O_CU, O_CV, O_CGATE, O_MERGE, IN_COLS) = _OFF

F32 = jnp.float32
BF16 = jnp.bfloat16

VMEM_LIMIT_BYTES = 56 * 1024 * 1024


def _lambda_init(layer_idx):
    return 0.8 - 0.6 * math.exp(-0.3 * layer_idx)


def _rms(x, g):
    return x * lax.rsqrt(jnp.mean(x * x, axis=-1, keepdims=True) + RMS_EPS) * g


def _dot(a, b):
    return jnp.dot(a, b, preferred_element_type=F32)


def _tiles(seq):
    tok = 256
    tq = min(seq, 4096)
    tqi = 256 if seq <= 2048 else 128
    tk = min(seq, 8192)
    heads = 8 if seq <= 2048 else 1
    body_steps = 4 if seq <= 2048 else 2
    return tok, tq, tqi, tk, heads, body_steps


def _pre_kernel(x_ref, xprev_ref, xnext_ref, g_ref, w_ref, cs_ref, sgug_ref, sguw_ref, sgub_ref,
                wc_ref, cw_ref, cb_ref, cg_ref, wa_ref,
                qt_ref, k_ref, vt_ref, gb_ref, pc_ref, m1_ref,
                mix_ref, abuf_ref, conv_ref, *, rows):
    tok = x_ref.shape[1]
    i = pl.program_id(1)
    last = pl.num_programs(1) - 1
    norm = lambda ref: _rms(ref[0], g_ref[...]).astype(BF16)
    h_ext = jnp.concatenate([norm(xprev_ref), norm(x_ref), norm(xnext_ref)], axis=0)
    h = h_ext[CONV_HALO:CONV_HALO + tok]

    def proj(c0, c1):
        return _dot(h, w_ref[:, c0:c1])

    za = _dot(h_ext, w_ref[:, O_AIN:O_AGATE])
    glu = za[:, :CONV_W] * jax.nn.sigmoid(za[:, CONV_W:])
    abuf_ref[0:CONV_HALO, :] = jnp.where(i > 0, glu[0:CONV_HALO], 0.0)
    abuf_ref[CONV_HALO:CONV_HALO + tok, :] = glu[CONV_HALO:CONV_HALO + tok]
    abuf_ref[CONV_HALO + tok:, :] = jnp.where(i < last, glu[CONV_HALO + tok:], 0.0)

    shift = CONV_HALO - CONV_K // 2
    win_rows = rows + 2 * CONV_HALO

    def conv_block(r0, c0):
        win = abuf_ref[r0:r0 + win_rows, c0:c0 + LANES]
        acc = jnp.zeros((rows // SUBLANES, SUBLANES, LANES), F32)
        for s in range(SUBLANES):
            ws = win if s == 0 else pltpu.roll(win, win_rows - s, 0)
            for a in range(win_rows // SUBLANES):
                k = a * SUBLANES + s - shift
                if 0 <= k < CONV_K:
                    tap = ws[a * SUBLANES:a * SUBLANES + rows].reshape(acc.shape)
                    acc = acc + cw_ref[k, :, c0:c0 + LANES] * tap
        conv_ref[r0:r0 + rows, c0:c0 + LANES] = acc.reshape(rows, LANES)

    pending = [(r0, c0) for r0 in range(0, tok, rows) for c0 in range(0, CONV_W, LANES)]
    share = -(-len(pending) // 8)

    def conv_some(count=share):
        for _ in range(min(count, len(pending))):
            conv_block(*pending.pop(0))

    cos = cs_ref[:, :HEAD_W]
    sin = cs_ref[:, HEAD_W:]
    lane = lax.broadcasted_iota(jnp.int32, (1, HEAD_W), 1)
    first_half = (lane % HEAD_DIM) < (HEAD_DIM // 2)

    def rotary(s):
        partner = jnp.where(first_half, pltpu.roll(s, HEAD_W - HEAD_DIM // 2, 1),
                            pltpu.roll(s, HEAD_DIM // 2, 1))
        return s * cos + partner * sin

    scale = math.log2(math.e) / math.sqrt(HEAD_DIM)
    zq = proj(O_Q, O_K)
    for hd in range(N_HEADS):
        s = zq[:, hd * HEAD_W:(hd + 1) * HEAD_W]
        qt_ref[0, hd] = (rotary(s) * scale).T.astype(BF16)
    conv_some()
    zk = proj(O_K, O_V)
    for hd in range(N_HEADS):
        s = zk[:, hd * HEAD_W:(hd + 1) * HEAD_W]
        k_ref[0, hd] = rotary(s).astype(BF16)
    conv_some()
    zv = proj(O_V, O_BGATE)
    for hd in range(N_HEADS):
        for c in range(tok // KEY_BLOCK):
            vt_ref[0, hd, c] = zv[c * KEY_BLOCK:(c + 1) * KEY_BLOCK,
                                  hd * HEAD_W:(hd + 1) * HEAD_W].T.astype(BF16)
    conv_some()
    gb_ref[0] = jax.nn.silu(proj(O_BGATE, O_CU)).astype(BF16)
    conv_some()

    cv = _rms(proj(O_CV, O_CGATE), sgug_ref[...]).astype(BF16)
    for n in range(tok // CHUNK):
        for g in range(SGU_GROUPS):
            blk = cv[n * CHUNK:(n + 1) * CHUNK, g * SGU_GROUP_W:(g + 1) * SGU_GROUP_W]
            mix_ref[n * CHUNK:(n + 1) * CHUNK, g * SGU_GROUP_W:(g + 1) * SGU_GROUP_W] = (
                _dot(sguw_ref[g], blk) + sgub_ref[:, g * SGU_GROUP_W:(g + 1) * SGU_GROUP_W])
    conv_some()
    u = proj(O_CU, O_CV) * mix_ref[...] * jax.nn.silu(proj(O_CGATE, O_MERGE))
    yc = _dot(u.astype(BF16), wc_ref[...])
    conv_some()

    m1_ref[0] = jax.nn.sigmoid(proj(O_MERGE + D_MODEL, O_MERGE + 2 * D_MODEL)).astype(BF16)
    conv_some()
    gated_c = jax.nn.sigmoid(proj(O_MERGE + 2 * D_MODEL, IN_COLS)) * yc
    conv_some(len(pending))
    a = jax.nn.silu(_rms(conv_ref[...] + cb_ref[...], cg_ref[...]))
    ya = _dot((a * jax.nn.silu(proj(O_AGATE, O_Q))).astype(BF16), wa_ref[...])
    pc_ref[0] = (jax.nn.sigmoid(proj(O_MERGE, O_MERGE + D_MODEL)) * ya + gated_c).astype(BF16)


def _pre_call(layer, x, g, w_in, cs, sgug, sguw, sgub, wc, cw, cb, cg, wa):
    b, seq, _ = x.shape
    tok = _tiles(seq)[0]
    halo_blocks = tok // CONV_HALO
    n_halo = seq // CONV_HALO
    grid = (b, seq // tok)
    lyr3 = lambda bi, i: (layer, 0, 0)
    tile = lambda bi, i: (bi, i, 0)
    head_tile = lambda bi, i: (bi, 0, i, 0)
    head_tile_t = lambda bi, i: (bi, 0, 0, i)
    single = pl.Buffered(1)
    in_specs = [
        pl.BlockSpec((1, tok, D_MODEL), tile),
        pl.BlockSpec((1, CONV_HALO, D_MODEL),
                     lambda bi, i: (bi, jnp.maximum(i * halo_blocks - 1, 0), 0)),
        pl.BlockSpec((1, CONV_HALO, D_MODEL),
                     lambda bi, i: (bi, jnp.minimum((i + 1) * halo_blocks, n_halo - 1), 0)),
        pl.BlockSpec((None, 1, D_MODEL), lyr3),
        pl.BlockSpec((None, D_MODEL, IN_COLS), lyr3, pipeline_mode=single),
        pl.BlockSpec((tok, 2 * HEAD_W), lambda bi, i: (i, 0)),
        pl.BlockSpec((None, 1, SGU_W), lyr3),
        pl.BlockSpec((None, SGU_GROUPS, CHUNK, CHUNK), lambda bi, i: (layer, 0, 0, 0)),
        pl.BlockSpec((None, CHUNK, SGU_W), lyr3),
        pl.BlockSpec((None, SGU_W, D_MODEL), lyr3),
        pl.BlockSpec((None, CONV_K, SUBLANES, CONV_W), lambda bi, i: (layer, 0, 0, 0)),
        pl.BlockSpec((None, 1, CONV_W), lyr3),
        pl.BlockSpec((None, 1, CONV_W), lyr3),
        pl.BlockSpec((None, CONV_W, D_MODEL), lyr3),
    ]
    out_shape = [
        jax.ShapeDtypeStruct((b, N_HEADS, HEAD_W, seq), BF16),
        jax.ShapeDtypeStruct((b, N_HEADS, seq, HEAD_W), BF16),
        jax.ShapeDtypeStruct((b, N_HEADS, seq // KEY_BLOCK, HEAD_W, KEY_BLOCK), BF16),
        jax.ShapeDtypeStruct((b, seq, ATTN_W), BF16),
        jax.ShapeDtypeStruct((b, seq, D_MODEL), BF16),
        jax.ShapeDtypeStruct((b, seq, D_MODEL), BF16),
    ]
    out_specs = [
        pl.BlockSpec((1, N_HEADS, HEAD_W, tok), head_tile_t),
        pl.BlockSpec((1, N_HEADS, tok, HEAD_W), head_tile),
        pl.BlockSpec((1, N_HEADS, tok // KEY_BLOCK, HEAD_W, KEY_BLOCK), lambda bi, i: (bi, 0, i, 0, 0)),
        pl.BlockSpec((1, tok, ATTN_W), tile),
        pl.BlockSpec((1, tok, D_MODEL), tile),
        pl.BlockSpec((1, tok, D_MODEL), tile),
    ]
    return pl.pallas_call(
        functools.partial(_pre_kernel, rows=32),
        grid=grid,
        in_specs=in_specs,
        out_specs=out_specs,
        out_shape=out_shape,
        scratch_shapes=[pltpu.VMEM((tok, SGU_W), F32),
                        pltpu.VMEM((tok + 2 * CONV_HALO, CONV_W), F32),
                        pltpu.VMEM((tok, CONV_W), F32)],
        compiler_params=pltpu.CompilerParams(
            dimension_semantics=("arbitrary", "arbitrary"),
            vmem_limit_bytes=VMEM_LIMIT_BYTES),
        name="pre",
    )(x, x, x, g, w_in, cs, sgug, sguw, sgub, wc, cw, cb, cg, wa)


def _attn_kernel(lam_ref, g_ref, qt_ref, k_ref, vt_ref, o_ref,
                 q2_ref, mref_ref, kmax_ref, p0_ref, p1_ref, l_ref, *maybe_acc_ref,
                 lam_init, tqi, tk, body_steps):
    heads, tq = qt_ref.shape[1], qt_ref.shape[3]
    seq = k_ref.shape[2]
    nq = tq // tqi
    n = seq // tk
    nb = heads * nq
    items = nb * n
    r = 2 * tqi
    assert n & (n - 1) == 0 and nq & (nq - 1) == 0 and items >= 2
    log2_n = n.bit_length() - 1
    log2_nq = nq.bit_length() - 1
    lam = (jnp.exp(jnp.sum(lam_ref[0:1, :] * lam_ref[1:2, :], axis=-1, keepdims=True))
           - jnp.exp(jnp.sum(lam_ref[2:3, :] * lam_ref[3:4, :], axis=-1, keepdims=True))
           + lam_init)

    feat = lax.broadcasted_iota(jnp.int32, (HEAD_W, 1), 0)
    is_map0 = feat < HEAD_DIM

    @pl.when(pl.program_id(2) == 0)
    def _():
        lane_map0 = lax.broadcasted_iota(jnp.int32, (1, HEAD_W), 1) < HEAD_DIM
        sel = jnp.where(is_map0 == lane_map0, 1.0, 0.0).astype(BF16)
        col_map0 = lax.broadcasted_iota(jnp.int32, (1, r), 1) < tqi

        for hd in range(heads):
            def chunk_max(kc, best):
                k0 = pl.multiple_of(kc * tk, tk)
                kf = k_ref[0, hd, pl.ds(k0, tk), :].astype(F32)
                return jnp.maximum(best, jnp.max(_dot((kf * kf).astype(BF16), sel),
                                                 axis=0, keepdims=True))

            best = lax.fori_loop(0, n, chunk_max, jnp.zeros((1, HEAD_W), F32))
            best0 = jnp.max(jnp.where(lane_map0, best, 0.0), axis=1, keepdims=True)
            best1 = jnp.max(jnp.where(lane_map0, 0.0, best), axis=1, keepdims=True)
            kmax_ref[hd] = jnp.where(col_map0, best0, best1)

    zero = jnp.zeros((), BF16)
    for qb in range(nb):
        hd, j = divmod(qb, nq)
        qt = qt_ref[0, hd, :, j * tqi:(j + 1) * tqi]
        q2 = jnp.concatenate([jnp.where(is_map0, qt, zero), jnp.where(is_map0, zero, qt)], axis=1)
        q2_ref[qb] = q2
        q2f = q2.astype(F32)
        qn2 = jnp.sum(q2f * q2f, axis=0, keepdims=True)
        mref_ref[qb] = jnp.sqrt(qn2 * kmax_ref[hd]) * BOUND_SLACK

    def split(t):
        return lax.shift_right_logical(t, log2_n), lax.bitwise_and(t, n - 1)

    def k_chunk(qb, kc):
        hd = lax.shift_right_logical(qb, log2_nq)
        return k_ref[0, hd, pl.ds(pl.multiple_of(kc * tk, tk), tk), :]

    def vt_chunk(qb, kc):
        hd = lax.shift_right_logical(qb, log2_nq)
        kb0 = pl.multiple_of(kc * (tk // KEY_BLOCK), tk // KEY_BLOCK)
        return jnp.concatenate([vt_ref[0, hd, kb0 + i] for i in range(tk // KEY_BLOCK)], axis=1)

    def stage_qs(t, l, p_ref):
        qi, kc = split(t)
        p = jnp.exp2(_dot(k_chunk(qi, kc), q2_ref[qi]) - mref_ref[qi])
        p_ref[...] = p.astype(BF16)
        l = jnp.where(kc == 0, 0.0, l) + jnp.sum(p, axis=0, keepdims=True)
        return l

    gain = g_ref[...] * (1.0 - lam_init)

    def store_output(qb, acc, l):
        o = acc[:, :tqi] / l[:, :tqi] - lam * (acc[:, tqi:] / l[:, tqi:])
        o = o * lax.rsqrt(jnp.mean(o * o, axis=0, keepdims=True) + RMS_EPS) * gain
        if isinstance(qb, int):
            hd, q0 = qb // nq, (qb % nq) * tqi
        else:
            hd = lax.shift_right_logical(qb, log2_nq)
            q0 = pl.multiple_of(lax.bitwise_and(qb, nq - 1) * tqi, tqi)
        o_ref[0, hd, pl.ds(q0, tqi), :] = o.T.astype(o_ref.dtype)

    def stage_p(t, l, acc, p_ref):
        qi, kc = split(t)
        acc = jnp.where(kc == 0, 0.0, acc) + _dot(vt_chunk(qi, kc), p_ref[...])
        l_ref[qi] = l
        if maybe_acc_ref:
            maybe_acc_ref[0][qi] = acc
        else:
            store_output(qi, acc, l)
        return acc

    p_refs = (p0_ref, p1_ref)
    l0 = stage_qs(0, jnp.zeros((1, r), F32), p0_ref)

    def step(tau, c, carry):
        l_prev, acc = carry
        acc = stage_p(tau, l_prev, acc, p_refs[c])
        return stage_qs(tau + 1, l_prev, p_refs[1 - c]), acc

    def body(u, carry):
        for j in range(body_steps):
            carry = step(u * body_steps + j, j % 2, carry)
        return carry

    assert body_steps % 2 == 0
    trips = (items - 1) // body_steps
    carry = lax.fori_loop(0, trips, body, (l0, jnp.zeros((HEAD_W, r), F32)))
    for tau in range(trips * body_steps, items - 1):
        carry = step(tau, tau % 2, carry)
    stage_p(items - 1, *carry, p_refs[(items - 1) % 2])

    if maybe_acc_ref:
        for qb in range(nb):
            store_output(qb, maybe_acc_ref[0][qb], l_ref[qb])

    @pl.when(jnp.logical_not(jnp.min(l_ref[...]) >= MIN_DENOMINATOR))
    def _():
        def exact_block(j, carry):
            def chunk(kc, mla):
                m, l, acc = mla
                s = _dot(k_chunk(j, kc), q2_ref[j])
                m_new = jnp.maximum(m, jnp.max(s, axis=0, keepdims=True))
                alpha = jnp.exp2(m - m_new)
                p = jnp.exp2(s - m_new)
                l = alpha * l + jnp.sum(p, axis=0, keepdims=True)
                acc = alpha * acc + _dot(vt_chunk(j, kc), p.astype(BF16))
                return m_new, l, acc

            init = (jnp.full((1, r), -jnp.inf, F32), jnp.zeros((1, r), F32),
                    jnp.zeros((HEAD_W, r), F32))
            _, l, acc = lax.fori_loop(0, n, chunk, init)
            store_output(j, acc, l)
            return carry

        lax.fori_loop(0, nb, exact_block, 0)


def _attn_call(layer, lam_vecs, subln_g, qt, k, vt, lam_init):
    b, _, seq, _ = k.shape
    _, tq, tqi, tk, heads, body_steps = _tiles(seq)
    nb = heads * (tq // tqi)
    r = 2 * tqi
    grid = (b, N_HEADS // heads, seq // tq)
    lyr3 = lambda bi, hd, i: (layer, 0, 0)
    return pl.pallas_call(
        functools.partial(_attn_kernel, lam_init=lam_init, tqi=tqi, tk=tk, body_steps=body_steps),
        grid=grid,
        in_specs=[
            pl.BlockSpec((None, 4, HEAD_DIM), lyr3),
            pl.BlockSpec((None, HEAD_W, 1), lyr3),
            pl.BlockSpec((1, heads, HEAD_W, tq), lambda bi, hd, i: (bi, hd, 0, i)),
            pl.BlockSpec((1, heads, seq, HEAD_W), lambda bi, hd, i: (bi, hd, 0, 0)),
            pl.BlockSpec((1, heads, seq // KEY_BLOCK, HEAD_W, KEY_BLOCK),
                         lambda bi, hd, i: (bi, hd, 0, 0, 0)),
        ],
        out_specs=pl.BlockSpec((1, heads, tq, HEAD_W), lambda bi, hd, i: (bi, hd, i, 0)),
        out_shape=jax.ShapeDtypeStruct((b, N_HEADS, seq, HEAD_W), BF16),
        scratch_shapes=[
            pltpu.VMEM((nb, HEAD_W, r), BF16),
            pltpu.VMEM((nb, 1, r), F32),
            pltpu.VMEM((heads, 1, r), F32),
            pltpu.VMEM((tk, r), BF16), pltpu.VMEM((tk, r), BF16),
            pltpu.VMEM((nb, 1, r), F32),
            *([pltpu.VMEM((nb, HEAD_W, r), F32)] if seq > tk else []),
        ],
        compiler_params=pltpu.CompilerParams(
            dimension_semantics=("arbitrary", "arbitrary", "arbitrary"),
            vmem_limit_bytes=VMEM_LIMIT_BYTES),
        name="attn",
    )(lam_vecs, subln_g, qt, k, vt)


def _post_kernel(x_ref, o_ref, gb_ref, pc_ref, m1_ref, wb_ref, wo_ref, fg_ref, y_ref, *, final):
    o = jnp.concatenate([o_ref[0, hd] for hd in range(N_HEADS)], axis=1)
    yb = _dot((o.astype(F32) * gb_ref[0].astype(F32)).astype(BF16), wb_ref[...])
    y = m1_ref[0].astype(F32) * yb + pc_ref[0].astype(F32)
    out = x_ref[0] + _dot(y.astype(BF16), wo_ref[...])
    if final:
        out = _rms(out, fg_ref[...])
    y_ref[0] = out


def _post_call(layer, x, o, gb, pc, m1, wb, wo, fg, final):
    b, seq, _ = x.shape
    tok = POST_TOKENS
    grid = (b, seq // tok)
    tile = lambda bi, i: (bi, i, 0)
    lyr3 = lambda bi, i: (layer, 0, 0)
    in_specs = [
        pl.BlockSpec((1, tok, D_MODEL), tile),
        pl.BlockSpec((1, N_HEADS, tok, HEAD_W), lambda bi, i: (bi, 0, i, 0)),
        pl.BlockSpec((1, tok, ATTN_W), tile),
        pl.BlockSpec((1, tok, D_MODEL), tile),
        pl.BlockSpec((1, tok, D_MODEL), tile),
        pl.BlockSpec((None, ATTN_W, D_MODEL), lyr3),
        pl.BlockSpec((None, D_MODEL, D_MODEL), lyr3),
        pl.BlockSpec((1, D_MODEL), lambda bi, i: (0, 0)),
    ]
    return pl.pallas_call(
        functools.partial(_post_kernel, final=final),
        grid=grid,
        in_specs=in_specs,
        out_specs=pl.BlockSpec((1, tok, D_MODEL), tile),
        out_shape=jax.ShapeDtypeStruct((b, seq, D_MODEL), F32),
        compiler_params=pltpu.CompilerParams(
            dimension_semantics=("arbitrary", "arbitrary"),
            vmem_limit_bytes=VMEM_LIMIT_BYTES),
        name="post",
    )(x, o, gb, pc, m1, wb, wo, fg)


def _rotary_table(seq):
    half = HEAD_DIM // 2
    inv = ROPE_THETA ** (-jnp.arange(half, dtype=F32) / half)
    ang = jnp.arange(seq, dtype=jnp.int32).astype(F32)[:, None] * inv[None, :]
    cos = jnp.tile(jnp.cos(ang), (1, 4))
    sin = jnp.tile(jnp.sin(ang), (1, 4))
    sign = jnp.where(jnp.arange(HEAD_W) % HEAD_DIM < half, -1.0, 1.0).astype(F32)
    return jnp.concatenate([cos, sin * sign[None, :]], axis=1)


def kernel(x_prompt, x_sample, norm_g, w_in, conv_w, conv_b, conv_norm_g, w_proj_a,
           lam_q1, lam_k1, lam_q2, lam_k2, subln_g, w_proj_b,
           sgu_norm_g, sgu_w, sgu_b, w_proj_c, w_out, final_g):
    w_in_b = w_in.astype(BF16)
    sgu_w_b = sgu_w.astype(BF16)
    sgu_bias = jnp.repeat(jnp.swapaxes(sgu_b, 1, 2), SGU_GROUP_W, axis=2)
    wc_b = w_proj_c.astype(BF16)
    conv_w_rep = jnp.broadcast_to(conv_w[:, :, None, :], (DEPTH, CONV_K, SUBLANES, CONV_W))
    wa_b = w_proj_a.astype(BF16)
    wb_b = w_proj_b.astype(BF16)
    wo_b = w_out.astype(BF16)
    lam_vecs = jnp.stack([lam_q1, lam_k1, lam_q2, lam_k2], axis=1).astype(F32)
    cs = _rotary_table(max(x_prompt.shape[1], x_sample.shape[1]))

    row = lambda p: p[:, None, :]
    outs = []
    for x in (x_prompt, x_sample):
        for l in range(DEPTH):
            qt, k, vt, gb, pc, m1 = _pre_call(
                l, x, row(norm_g), w_in_b, cs, row(sgu_norm_g), sgu_w_b, sgu_bias, wc_b,
                conv_w_rep, row(conv_b), row(conv_norm_g), wa_b)
            o = _attn_call(l, lam_vecs, subln_g[:, :, None], qt, k, vt, _lambda_init(l))
            x = _post_call(l, x, o, gb, pc, m1, wb_b, wo_b, final_g[None],
                           final=(l == DEPTH - 1))
        outs.append(x)
    return tuple(outs)
```

```python
import functools
import math

import jax
import jax.numpy as jnp
from jax import lax
from jax.experimental import pallas as pl
from jax.experimental.pallas import tpu as pltpu

D_MODEL = 1024
DEPTH = 2
CONV_W = 512
CONV_K = 31
CONV_HALO = 16
SUBLANES = 8
LANES = 128
N_HEADS = 8
HEAD_DIM = 64
HEAD_W = 2 * HEAD_DIM
ATTN_W = N_HEADS * HEAD_W
ROPE_THETA = 10000.0
SGU_W = 512
SGU_GROUPS = 4
SGU_GROUP_W = SGU_W // SGU_GROUPS
CHUNK = 128
N_BRANCH = 3
RMS_EPS = 1e-6
POST_TOKENS = 512
KEY_BLOCK = 128
BOUND_SLACK = 1.01
MIN_DENOMINATOR = 2.0 ** -64

_COLS = (2 * CONV_W, CONV_W, ATTN_W, ATTN_W, ATTN_W, ATTN_W, SGU_W, SGU_W, SGU_W, N_BRANCH * D_MODEL)
_OFF = [0]
for _c in _COLS:
    _OFF.append(_OFF[-1] + _c)
(O_AIN, O_AGATE, O_Q, O_K, O_V, O_BGATE, O_CU, O_CV, O_CGATE, O_MERGE, IN_COLS) = _OFF

F32 = jnp.float32
BF16 = jnp.bfloat16

VMEM_LIMIT_BYTES = 56 * 1024 * 1024


def _lambda_init(layer_idx):
    return 0.8 - 0.6 * math.exp(-0.3 * layer_idx)


def _rms(x, g):
    return x * lax.rsqrt(jnp.mean(x * x, axis=-1, keepdims=True) + RMS_EPS) * g


def _dot(a, b):
    return jnp.dot(a, b, preferred_element_type=F32)


def _tiles(seq):
    tok = 256
    tq = min(seq, 4096)
    tqi = 256
    tk = min(seq, 4096)
    heads = 8 if seq <= 2048 else 1
    body_steps = 4
    return tok, tq, tqi, tk, heads, body_steps


def _pre_kernel(x_ref, xprev_ref, xnext_ref, g_ref, w_ref, cs_ref, sgug_ref, sguw_ref, sgub_ref,
                wc_ref, cw_ref, cb_ref, cg_ref, wa_ref,
                qt_ref, k_ref, vt_ref, gb_ref, pc_ref, m1_ref,
                mix_ref, abuf_ref, conv_ref, *, rows):
    tok = x_ref.shape[1]
    i = pl.program_id(1)
    last = pl.num_programs(1) - 1
    norm = lambda ref: _rms(ref[0], g_ref[...]).astype(BF16)
    h_ext = jnp.concatenate([norm(xprev_ref), norm(x_ref), norm(xnext_ref)], axis=0)
    h = h_ext[CONV_HALO:CONV_HALO + tok]

    def proj(c0, c1):
        return _dot(h, w_ref[:, c0:c1])

    za = _dot(h_ext, w_ref[:, O_AIN:O_AGATE])
    glu = za[:, :CONV_W] * jax.nn.sigmoid(za[:, CONV_W:])
    abuf_ref[0:CONV_HALO, :] = jnp.where(i > 0, glu[0:CONV_HALO], 0.0)
    abuf_ref[CONV_HALO:CONV_HALO + tok, :] = glu[CONV_HALO:CONV_HALO + tok]
    abuf_ref[CONV_HALO + tok:, :] = jnp.where(i < last, glu[CONV_HALO + tok:], 0.0)

    shift = CONV_HALO - CONV_K // 2
    win_rows = rows + 2 * CONV_HALO

    def conv_block(r0, c0):
        win = abuf_ref[r0:r0 + win_rows, c0:c0 + LANES]
        acc = jnp.zeros((rows // SUBLANES, SUBLANES, LANES), F32)
        for s in range(SUBLANES):
            ws = win if s == 0 else pltpu.roll(win, win_rows - s, 0)
            for a in range(win_rows // SUBLANES):
                k = a * SUBLANES + s - shift
                if 0 <= k < CONV_K:
                    tap = ws[a * SUBLANES:a * SUBLANES + rows].reshape(acc.shape)
                    acc = acc + cw_ref[k, :, c0:c0 + LANES] * tap
        conv_ref[r0:r0 + rows, c0:c0 + LANES] = acc.reshape(rows, LANES)

    pending = [(r0, c0) for r0 in range(0, tok, rows) for c0 in range(0, CONV_W, LANES)]
    share = -(-len(pending) // 8)

    def conv_some(count=share):
        for _ in range(min(count, len(pending))):
            conv_block(*pending.pop(0))

    cos = cs_ref[:, :HEAD_W]
    sin = cs_ref[:, HEAD_W:]
    lane = lax.broadcasted_iota(jnp.int32, (1, HEAD_W), 1)
    first_half = (lane % HEAD_DIM) < (HEAD_DIM // 2)

    def rotary(s):
        partner = jnp.where(first_half, pltpu.roll(s, HEAD_W - HEAD_DIM // 2, 1),
                            pltpu.roll(s, HEAD_DIM // 2, 1))
        return s * cos + partner * sin

    scale = math.log2(math.e) / math.sqrt(HEAD_DIM)
    zq = proj(O_Q, O_K)
    for hd in range(N_HEADS):
        s = zq[:, hd * HEAD_W:(hd + 1) * HEAD_W]
        qt_ref[0, hd] = (rotary(s) * scale).T.astype(BF16)
    conv_some()
    zk = proj(O_K, O_V)
    for hd in range(N_HEADS):
        s = zk[:, hd * HEAD_W:(hd + 1) * HEAD_W]
        k_ref[0, hd] = rotary(s).astype(BF16)
    conv_some()
    zv = proj(O_V, O_BGATE)
    for hd in range(N_HEADS):
        for c in range(tok // KEY_BLOCK):
            vt_ref[0, hd, c] = zv[c * KEY_BLOCK:(c + 1) * KEY_BLOCK,
                                  hd * HEAD_W:(hd + 1) * HEAD_W].T.astype(BF16)
    conv_some()
    gb_ref[0] = jax.nn.silu(proj(O_BGATE, O_CU)).astype(BF16)
    conv_some()

    cv = _rms(proj(O_CV, O_CGATE), sgug_ref[...]).astype(BF16)
    for n in range(tok // CHUNK):
        for g in range(SGU_GROUPS):
            blk = cv[n * CHUNK:(n + 1) * CHUNK, g * SGU_GROUP_W:(g + 1) * SGU_GROUP_W]
            mix_ref[n * CHUNK:(n + 1) * CHUNK, g * SGU_GROUP_W:(g + 1) * SGU_GROUP_W] = (
                _dot(sguw_ref[g], blk) + sgub_ref[:, g * SGU_GROUP_W:(g + 1) * SGU_GROUP_W])
    conv_some()
    u = proj(O_CU, O_CV) * mix_ref[...] * jax.nn.silu(proj(O_CGATE, O_MERGE))
    yc = _dot(u.astype(BF16), wc_ref[...])
    conv_some()

    m1_ref[0] = jax.nn.sigmoid(proj(O_MERGE + D_MODEL, O_MERGE + 2 * D_MODEL)).astype(BF16)
    conv_some()
    gated_c = jax.nn.sigmoid(proj(O_MERGE + 2 * D_MODEL, IN_COLS)) * yc
    conv_some(len(pending))
    a = jax.nn.silu(_rms(conv_ref[...] + cb_ref[...], cg_ref[...]))
    ya = _dot((a * jax.nn.silu(proj(O_AGATE, O_Q))).astype(BF16), wa_ref[...])
    pc_ref[0] = (jax.nn.sigmoid(proj(O_MERGE, O_MERGE + D_MODEL)) * ya + gated_c).astype(BF16)


def _pre_call(layer, x, g, w_in, cs, sgug, sguw, sgub, wc, cw, cb, cg, wa):
    b, seq, _ = x.shape
    tok = _tiles(seq)[0]
    halo_blocks = tok // CONV_HALO
    n_halo = seq // CONV_HALO
    grid = (b, seq // tok)
    lyr3 = lambda bi, i: (layer, 0, 0)
    tile = lambda bi, i: (bi, i, 0)
    head_tile = lambda bi, i: (bi, 0, i, 0)
    head_tile_t = lambda bi, i: (bi, 0, 0, i)
    single = pl.Buffered(1)
    in_specs = [
        pl.BlockSpec((1, tok, D_MODEL), tile),
        pl.BlockSpec((1, CONV_HALO, D_MODEL),
                     lambda bi, i: (bi, jnp.maximum(i * halo_blocks - 1, 0), 0)),
        pl.BlockSpec((1, CONV_HALO, D_MODEL),
                     lambda bi, i: (bi, jnp.minimum((i + 1) * halo_blocks, n_halo - 1), 0)),
        pl.BlockSpec((None, 1, D_MODEL), lyr3),
        pl.BlockSpec((None, D_MODEL, IN_COLS), lyr3, pipeline_mode=single),
        pl.BlockSpec((tok, 2 * HEAD_W), lambda bi, i: (i, 0)),
        pl.BlockSpec((None, 1, SGU_W), lyr3),
        pl.BlockSpec((None, SGU_GROUPS, CHUNK, CHUNK), lambda bi, i: (layer, 0, 0, 0)),
        pl.BlockSpec((None, CHUNK, SGU_W), lyr3),
        pl.BlockSpec((None, SGU_W, D_MODEL), lyr3),
        pl.BlockSpec((None, CONV_K, SUBLANES, CONV_W), lambda bi, i: (layer, 0, 0, 0)),
        pl.BlockSpec((None, 1, CONV_W), lyr3),
        pl.BlockSpec((None, 1, CONV_W), lyr3),
        pl.BlockSpec((None, CONV_W, D_MODEL), lyr3),
    ]
    out_shape = [
        jax.ShapeDtypeStruct((b, N_HEADS, HEAD_W, seq), BF16),
        jax.ShapeDtypeStruct((b, N_HEADS, seq, HEAD_W), BF16),
        jax.ShapeDtypeStruct((b, N_HEADS, seq // KEY_BLOCK, HEAD_W, KEY_BLOCK), BF16),
        jax.ShapeDtypeStruct((b, seq, ATTN_W), BF16),
        jax.ShapeDtypeStruct((b, seq, D_MODEL), BF16),
        jax.ShapeDtypeStruct((b, seq, D_MODEL), BF16),
    ]
    out_specs = [
        pl.BlockSpec((1, N_HEADS, HEAD_W, tok), head_tile_t),
        pl.BlockSpec((1, N_HEADS, tok, HEAD_W), head_tile),
        pl.BlockSpec((1, N_HEADS, tok // KEY_BLOCK, HEAD_W, KEY_BLOCK), lambda bi, i: (bi, 0, i, 0, 0)),
        pl.BlockSpec((1, tok, ATTN_W), tile),
        pl.BlockSpec((1, tok, D_MODEL), tile),
        pl.BlockSpec((1, tok, D_MODEL), tile),
    ]
    return pl.pallas_call(
        functools.partial(_pre_kernel, rows=32),
        grid=grid,
        in_specs=in_specs,
        out_specs=out_specs,
        out_shape=out_shape,
        scratch_shapes=[pltpu.VMEM((tok, SGU_W), F32),
                        pltpu.VMEM((tok + 2 * CONV_HALO, CONV_W), F32),
                        pltpu.VMEM((tok, CONV_W), F32)],
        compiler_params=pltpu.CompilerParams(
            dimension_semantics=("arbitrary", "arbitrary"),
            vmem_limit_bytes=VMEM_LIMIT_BYTES),
        name="pre",
    )(x, x, x, g, w_in, cs, sgug, sguw, sgub, wc, cw, cb, cg, wa)


def _attn_kernel(lam_ref, g_ref, qt_ref, k_ref, vt_ref, o_ref,
                 q2_ref, mref_ref, kmax_ref, p0_ref, p1_ref, l_ref, *maybe_acc_ref,
                 lam_init, tqi, tk, body_steps):
    heads, tq = qt_ref.shape[1], qt_ref.shape[3]
    seq = k_ref.shape[2]
    nq = tq // tqi
    n = seq // tk
    nb = heads * nq
    items = nb * n
    r = 2 * tqi
    assert n & (n - 1) == 0 and nq & (nq - 1) == 0 and items >= 2
    log2_n = n.bit_length() - 1
    log2_nq = nq.bit_length() - 1
    lam = (jnp.exp(jnp.sum(lam_ref[0:1, :] * lam_ref[1:2, :], axis=-1, keepdims=True))
           - jnp.exp(jnp.sum(lam_ref[2:3, :] * lam_ref[3:4, :], axis=-1, keepdims=True))
           + lam_init)

    feat = lax.broadcasted_iota(jnp.int32, (HEAD_W, 1), 0)
    is_map0 = feat < HEAD_DIM

    @pl.when(pl.program_id(2) == 0)
    def _():
        lane_map0 = lax.broadcasted_iota(jnp.int32, (1, HEAD_W), 1) < HEAD_DIM
        sel = jnp.where(is_map0 == lane_map0, 1.0, 0.0).astype(BF16)
        col_map0 = lax.broadcasted_iota(jnp.int32, (1, r), 1) < tqi

        for hd in range(heads):
            def chunk_max(kc, best):
                k0 = pl.multiple_of(kc * tk, tk)
                kf = k_ref[0, hd, pl.ds(k0, tk), :].astype(F32)
                return jnp.maximum(best, jnp.max(_dot((kf * kf).astype(BF16), sel),
                                                 axis=0, keepdims=True))

            best = lax.fori_loop(0, n, chunk_max, jnp.zeros((1, HEAD_W), F32))
            best0 = jnp.max(jnp.where(lane_map0, best, 0.0), axis=1, keepdims=True)
            best1 = jnp.max(jnp.where(lane_map0, 0.0, best), axis=1, keepdims=True)
            kmax_ref[hd] = jnp.where(col_map0, best0, best1)

    zero = jnp.zeros((), BF16)
    for qb in range(nb):
        hd, j = divmod(qb, nq)
        qt = qt_ref[0, hd, :, j * tqi:(j + 1) * tqi]
        q2 = jnp.concatenate([jnp.where(is_map0, qt, zero), jnp.where(is_map0, zero, qt)], axis=1)
        q2_ref[qb] = q2
        q2f = q2.astype(F32)
        qn2 = jnp.sum(q2f * q2f, axis=0, keepdims=True)
        mref_ref[qb] = jnp.sqrt(qn2 * kmax_ref[hd]) * BOUND_SLACK

    def split(t):
        return lax.shift_right_logical(t, log2_n), lax.bitwise_and(t, n - 1)

    def k_chunk(qb, kc):
        hd = lax.shift_right_logical(qb, log2_nq)
        return k_ref[0, hd, pl.ds(pl.multiple_of(kc * tk, tk), tk), :]

    def vt_chunk(qb, kc):
        hd = lax.shift_right_logical(qb, log2_nq)
        kb0 = pl.multiple_of(kc * (tk // KEY_BLOCK), tk // KEY_BLOCK)
        return jnp.concatenate([vt_ref[0, hd, kb0 + i] for i in range(tk // KEY_BLOCK)], axis=1)

    def stage_qs(t, l, p_ref):
        qi, kc = split(t)
        p = jnp.exp2(_dot(k_chunk(qi, kc), q2_ref[qi]) - mref_ref[qi])
        p_ref[...] = p.astype(BF16)
        l = jnp.where(kc == 0, 0.0, l) + jnp.sum(p, axis=0, keepdims=True)
        return l

    gain = g_ref[...] * (1.0 - lam_init)

    def store_output(qb, acc, l):
        o = acc[:, :tqi] / l[:, :tqi] - lam * (acc[:, tqi:] / l[:, tqi:])
        o = o * lax.rsqrt(jnp.mean(o * o, axis=0, keepdims=True) + RMS_EPS) * gain
        if isinstance(qb, int):
            hd, q0 = qb // nq, (qb % nq) * tqi
        else:
            hd = lax.shift_right_logical(qb, log2_nq)
            q0 = pl.multiple_of(lax.bitwise_and(qb, nq - 1) * tqi, tqi)
        o_ref[0, hd, pl.ds(q0, tqi), :] = o.T.astype(o_ref.dtype)

    def stage_p(t, l, acc, p_ref):
        qi, kc = split(t)
        acc = jnp.where(kc == 0, 0.0, acc) + _dot(vt_chunk(qi, kc), p_ref[...])
        l_ref[qi] = l
        if maybe_acc_ref:
            maybe_acc_ref[0][qi] = acc
        else:
            store_output(qi, acc, l)
        return acc

    p_refs = (p0_ref, p1_ref)
    l0 = stage_qs(0, jnp.zeros((1, r), F32), p0_ref)

    def step(tau, c, carry):
        l_prev, acc = carry
        acc = stage_p(tau, l_prev, acc, p_refs[c])
        return stage_qs(tau + 1, l_prev, p_refs[1 - c]), acc

    def body(u, carry):
        for j in range(body_steps):
            carry = step(u * body_steps + j, j % 2, carry)
        return carry

    assert body_steps % 2 == 0
    trips = (items - 1) // body_steps
    carry = lax.fori_loop(0, trips, body, (l0, jnp.zeros((HEAD_W, r), F32)))
    for tau in range(trips * body_steps, items - 1):
        carry = step(tau, tau % 2, carry)
    stage_p(items - 1, *carry, p_refs[(items - 1) % 2])

    if maybe_acc_ref:
        for qb in range(nb):
            store_output(qb, maybe_acc_ref[0][qb], l_ref[qb])

    @pl.when(jnp.logical_not(jnp.min(l_ref[...]) >= MIN_DENOMINATOR))
    def _():
        def exact_block(j, carry):
            def chunk(kc, mla):
                m, l, acc = mla
                s = _dot(k_chunk(j, kc), q2_ref[j])
                m_new = jnp.maximum(m, jnp.max(s, axis=0, keepdims=True))
                alpha = jnp.exp2(m - m_new)
                p = jnp.exp2(s - m_new)
                l = alpha * l + jnp.sum(p, axis=0, keepdims=True)
                acc = alpha * acc + _dot(vt_chunk(j, kc), p.astype(BF16))
                return m_new, l, acc

            init = (jnp.full((1, r), -jnp.inf, F32), jnp.zeros((1, r), F32),
                    jnp.zeros((HEAD_W, r), F32))
            _, l, acc = lax.fori_loop(0, n, chunk, init)
            store_output(j, acc, l)
            return carry

        lax.fori_loop(0, nb, exact_block, 0)


def _attn_call(layer, lam_vecs, subln_g, qt, k, vt, lam_init):
    b, _, seq, _ = k.shape
    _, tq, tqi, tk, heads, body_steps = _tiles(seq)
    nb = heads * (tq // tqi)
    r = 2 * tqi
    grid = (b, N_HEADS // heads, seq // tq)
    lyr3 = lambda bi, hd, i: (layer, 0, 0)
    return pl.pallas_call(
        functools.partial(_attn_kernel, lam_init=lam_init, tqi=tqi, tk=tk, body_steps=body_steps),
        grid=grid,
        in_specs=[
            pl.BlockSpec((None, 4, HEAD_DIM), lyr3),
            pl.BlockSpec((None, HEAD_W, 1), lyr3),
            pl.BlockSpec((1, heads, HEAD_W, tq), lambda bi, hd, i: (bi, hd, 0, i)),
            pl.BlockSpec((1, heads, seq, HEAD_W), lambda bi, hd, i: (bi, hd, 0, 0)),
            pl.BlockSpec((1, heads, seq // KEY_BLOCK, HEAD_W, KEY_BLOCK),
                         lambda bi, hd, i: (bi, hd, 0, 0, 0)),
        ],
        out_specs=pl.BlockSpec((1, heads, tq, HEAD_W), lambda bi, hd, i: (bi, hd, i, 0)),
        out_shape=jax.ShapeDtypeStruct((b, N_HEADS, seq, HEAD_W), BF16),
        scratch_shapes=[
            pltpu.VMEM((nb, HEAD_W, r), BF16),
            pltpu.VMEM((nb, 1, r), F32),
            pltpu.VMEM((heads, 1, r), F32),
            pltpu.VMEM((tk, r), BF16), pltpu.VMEM((tk, r), BF16),
            pltpu.VMEM((nb, 1, r), F32),
            *([pltpu.VMEM((nb, HEAD_W, r), F32)] if seq > tk else []),
        ],
        compiler_params=pltpu.CompilerParams(
            dimension_semantics=("arbitrary", "arbitrary", "arbitrary"),
            vmem_limit_bytes=VMEM_LIMIT_BYTES),
        name="attn",
    )(lam_vecs, subln_g, qt, k, vt)


def _post_kernel(x_ref, o_ref, gb_ref, pc_ref, m1_ref, wb_ref, wo_ref, fg_ref, y_ref, *, final):
    o = jnp.concatenate([o_ref[0, hd] for hd in range(N_HEADS)], axis=1)
    yb = _dot((o.astype(F32) * gb_ref[0].astype(F32)).astype(BF16), wb_ref[...])
    y = m1_ref[0].astype(F32) * yb + pc_ref[0].astype(F32)
    out = x_ref[0] + _dot(y.astype(BF16), wo_ref[...])
    if final:
        out = _rms(out, fg_ref[...])
    y_ref[0] = out


def _post_call(layer, x, o, gb, pc, m1, wb, wo, fg, final):
    b, seq, _ = x.shape
    tok = POST_TOKENS
    grid = (b, seq // tok)
    tile = lambda bi, i: (bi, i, 0)
    lyr3 = lambda bi, i: (layer, 0, 0)
    in_specs = [
        pl.BlockSpec((1, tok, D_MODEL), tile),
        pl.BlockSpec((1, N_HEADS, tok, HEAD_W), lambda bi, i: (bi, 0, i, 0)),
        pl.BlockSpec((1, tok, ATTN_W), tile),
        pl.BlockSpec((1, tok, D_MODEL), tile),
        pl.BlockSpec((1, tok, D_MODEL), tile),
        pl.BlockSpec((None, ATTN_W, D_MODEL), lyr3),
        pl.BlockSpec((None, D_MODEL, D_MODEL), lyr3),
        pl.BlockSpec((1, D_MODEL), lambda bi, i: (0, 0)),
    ]
    return pl.pallas_call(
        functools.partial(_post_kernel, final=final),
        grid=grid,
        in_specs=in_specs,
        out_specs=pl.BlockSpec((1, tok, D_MODEL), tile),
        out_shape=jax.ShapeDtypeStruct((b, seq, D_MODEL), F32),
        compiler_params=pltpu.CompilerParams(
            dimension_semantics=("arbitrary", "arbitrary"),
            vmem_limit_bytes=VMEM_LIMIT_BYTES),
        name="post",
    )(x, o, gb, pc, m1, wb, wo, fg)


def _rotary_table(seq):
    half = HEAD_DIM // 2
    inv = ROPE_THETA ** (-jnp.arange(half, dtype=F32) / half)
    ang = jnp.arange(seq, dtype=jnp.int32).astype(F32)[:, None] * inv[None, :]
    cos = jnp.tile(jnp.cos(ang), (1, 4))
    sin = jnp.tile(jnp.sin(ang), (1, 4))
    sign = jnp.where(jnp.arange(HEAD_W) % HEAD_DIM < half, -1.0, 1.0).astype(F32)
    return jnp.concatenate([cos, sin * sign[None, :]], axis=1)


def kernel(x_prompt, x_sample, norm_g, w_in, conv_w, conv_b, conv_norm_g, w_proj_a,
           lam_q1, lam_k1, lam_q2, lam_k2, subln_g, w_proj_b,
           sgu_norm_g, sgu_w, sgu_b, w_proj_c, w_out, final_g):
    w_in_b = w_in.astype(BF16)
    sgu_w_b = sgu_w.astype(BF16)
    sgu_bias = jnp.repeat(jnp.swapaxes(sgu_b, 1, 2), SGU_GROUP_W, axis=2)
    wc_b = w_proj_c.astype(BF16)
    conv_w_rep = jnp.broadcast_to(conv_w[:, :, None, :], (DEPTH, CONV_K, SUBLANES, CONV_W))
    wa_b = w_proj_a.astype(BF16)
    wb_b = w_proj_b.astype(BF16)
    wo_b = w_out.astype(BF16)
    lam_vecs = jnp.stack([lam_q1, lam_k1, lam_q2, lam_k2], axis=1).astype(F32)
    cs = _rotary_table(max(x_prompt.shape[1], x_sample.shape[1]))

    row = lambda p: p[:, None, :]
    outs = []
    for x in (x_prompt, x_sample):
        for l in range(DEPTH):
            qt, k, vt, gb, pc, m1 = _pre_call(
                l, x, row(norm_g), w_in_b, cs, row(sgu_norm_g), sgu_w_b, sgu_bias, wc_b,
                conv_w_rep, row(conv_b), row(conv_norm_g), wa_b)
            o = _attn_call(l, lam_vecs, subln_g[:, :, None], qt, k, vt, _lambda_init(l))
            x = _post_call(l, x, o, gb, pc, m1, wb_b, wo_b, final_g[None],
                           final=(l == DEPTH - 1))
        outs.append(x)
    return tuple(outs)
```

```python
import functools
import math

import jax
import jax.numpy as jnp
from jax import lax
from jax.experimental import pallas as pl
from jax.experimental.pallas import tpu as pltpu

D_MODEL = 1024
DEPTH = 2
CONV_W = 512
CONV_K = 31
CONV_HALO = 16
SUBLANES = 8
LANES = 128
N_HEADS = 8
HEAD_DIM = 64
HEAD_W = 2 * HEAD_DIM
ATTN_W = N_HEADS * HEAD_W
ROPE_THETA = 10000.0
SGU_W = 512
SGU_GROUPS = 4
SGU_GROUP_W = SGU_W // SGU_GROUPS
CHUNK = 128
N_BRANCH = 3
RMS_EPS = 1e-6
POST_TOKENS = 512
KEY_BLOCK = 128
BOUND_SLACK = 1.01
MIN_DENOMINATOR = 2.0 ** -64

_COLS = (2 * CONV_W, CONV_W, ATTN_W, ATTN_W, ATTN_W, ATTN_W, SGU_W, SGU_W, SGU_W, N_BRANCH * D_MODEL)
_OFF = [0]
for _c in _COLS:
    _OFF.append(_OFF[-1] + _c)
(O_AIN, O_AGATE, O_Q, O_K, O_V, O_BGATE, O_CU, O_CV, O_CGATE, O_MERGE, IN_COLS) = _OFF

F32 = jnp.float32
BF16 = jnp.bfloat16

VMEM_LIMIT_BYTES = 56 * 1024 * 1024


def _lambda_init(layer_idx):
    return 0.8 - 0.6 * math.exp(-0.3 * layer_idx)


def _rms(x, g):
    return x * lax.rsqrt(jnp.mean(x * x, axis=-1, keepdims=True) + RMS_EPS) * g


def _dot(a, b):
    return jnp.dot(a, b, preferred_element_type=F32)


def _tiles(seq):
    tok = 256
    tq = min(seq, 4096)
    tqi = 256
    tk = min(seq, 4096)
    heads = 8 if seq <= 2048 else 1
    body_steps = 8
    return tok, tq, tqi, tk, heads, body_steps


def _pre_kernel(x_ref, xprev_ref, xnext_ref, g_ref, w_ref, cs_ref, sgug_ref, sguw_ref, sgub_ref,
                wc_ref, cw_ref, cb_ref, cg_ref, wa_ref,
                qt_ref, k_ref, vt_ref, gb_ref, pc_ref, m1_ref,
                mix_ref, abuf_ref, conv_ref, *, rows):
    tok = x_ref.shape[1]
    i = pl.program_id(1)
    last = pl.num_programs(1) - 1
    norm = lambda ref: _rms(ref[0], g_ref[...]).astype(BF16)
    h_ext = jnp.concatenate([norm(xprev_ref), norm(x_ref), norm(xnext_ref)], axis=0)
    h = h_ext[CONV_HALO:CONV_HALO + tok]

    def proj(c0, c1):
        return _dot(h, w_ref[:, c0:c1])

    za = _dot(h_ext, w_ref[:, O_AIN:O_AGATE])
    glu = za[:, :CONV_W] * jax.nn.sigmoid(za[:, CONV_W:])
    abuf_ref[0:CONV_HALO, :] = jnp.where(i > 0, glu[0:CONV_HALO], 0.0)
    abuf_ref[CONV_HALO:CONV_HALO + tok, :] = glu[CONV_HALO:CONV_HALO + tok]
    abuf_ref[CONV_HALO + tok:, :] = jnp.where(i < last, glu[CONV_HALO + tok:], 0.0)

    shift = CONV_HALO - CONV_K // 2
    win_rows = rows + 2 * CONV_HALO

    def conv_block(r0, c0):
        win = abuf_ref[r0:r0 + win_rows, c0:c0 + LANES]
        acc = jnp.zeros((rows // SUBLANES, SUBLANES, LANES), F32)
        for s in range(SUBLANES):
            ws = win if s == 0 else pltpu.roll(win, win_rows - s, 0)
            for a in range(win_rows // SUBLANES):
                k = a * SUBLANES + s - shift
                if 0 <= k < CONV_K:
                    tap = ws[a * SUBLANES:a * SUBLANES + rows].reshape(acc.shape)
                    acc = acc + cw_ref[k, :, c0:c0 + LANES] * tap
        conv_ref[r0:r0 + rows, c0:c0 + LANES] = acc.reshape(rows, LANES)

    pending = [(r0, c0) for r0 in range(0, tok, rows) for c0 in range(0, CONV_W, LANES)]
    share = -(-len(pending) // 8)

    def conv_some(count=share):
        for _ in range(min(count, len(pending))):
            conv_block(*pending.pop(0))

    cos = cs_ref[:, :HEAD_W]
    sin = cs_ref[:, HEAD_W:]
    lane = lax.broadcasted_iota(jnp.int32, (1, HEAD_W), 1)
    first_half = (lane % HEAD_DIM) < (HEAD_DIM // 2)

    def rotary(s):
        partner = jnp.where(first_half, pltpu.roll(s, HEAD_W - HEAD_DIM // 2, 1),
                            pltpu.roll(s, HEAD_DIM // 2, 1))
        return s * cos + partner * sin

    scale = math.log2(math.e) / math.sqrt(HEAD_DIM)
    zq = proj(O_Q, O_K)
    for hd in range(N_HEADS):
        s = zq[:, hd * HEAD_W:(hd + 1) * HEAD_W]
        qt_ref[0, hd] = (rotary(s) * scale).T.astype(BF16)
    conv_some()
    zk = proj(O_K, O_V)
    for hd in range(N_HEADS):
        s = zk[:, hd * HEAD_W:(hd + 1) * HEAD_W]
        k_ref[0, hd] = rotary(s).astype(BF16)
    conv_some()
    zv = proj(O_V, O_BGATE)
    for hd in range(N_HEADS):
        for c in range(tok // KEY_BLOCK):
            vt_ref[0, hd, c] = zv[c * KEY_BLOCK:(c + 1) * KEY_BLOCK,
                                  hd * HEAD_W:(hd + 1) * HEAD_W].T.astype(BF16)
    conv_some()
    gb_ref[0] = jax.nn.silu(proj(O_BGATE, O_CU)).astype(BF16)
    conv_some()

    cv = _rms(proj(O_CV, O_CGATE), sgug_ref[...]).astype(BF16)
    for n in range(tok // CHUNK):
        for g in range(SGU_GROUPS):
            blk = cv[n * CHUNK:(n + 1) * CHUNK, g * SGU_GROUP_W:(g + 1) * SGU_GROUP_W]
            mix_ref[n * CHUNK:(n + 1) * CHUNK, g * SGU_GROUP_W:(g + 1) * SGU_GROUP_W] = (
                _dot(sguw_ref[g], blk) + sgub_ref[:, g * SGU_GROUP_W:(g + 1) * SGU_GROUP_W])
    conv_some()
    u = proj(O_CU, O_CV) * mix_ref[...] * jax.nn.silu(proj(O_CGATE, O_MERGE))
    yc = _dot(u.astype(BF16), wc_ref[...])
    conv_some()

    m1_ref[0] = jax.nn.sigmoid(proj(O_MERGE + D_MODEL, O_MERGE + 2 * D_MODEL)).astype(BF16)
    conv_some()
    gated_c = jax.nn.sigmoid(proj(O_MERGE + 2 * D_MODEL, IN_COLS)) * yc
    conv_some(len(pending))
    a = jax.nn.silu(_rms(conv_ref[...] + cb_ref[...], cg_ref[...]))
    ya = _dot((a * jax.nn.silu(proj(O_AGATE, O_Q))).astype(BF16), wa_ref[...])
    pc_ref[0] = (jax.nn.sigmoid(proj(O_MERGE, O_MERGE + D_MODEL)) * ya + gated_c).astype(BF16)


def _pre_call(layer, x, g, w_in, cs, sgug, sguw, sgub, wc, cw, cb, cg, wa):
    b, seq, _ = x.shape
    tok = _tiles(seq)[0]
    halo_blocks = tok // CONV_HALO
    n_halo = seq // CONV_HALO
    grid = (b, seq // tok)
    lyr3 = lambda bi, i: (layer, 0, 0)
    tile = lambda bi, i: (bi, i, 0)
    head_tile = lambda bi, i: (bi, 0, i, 0)
    head_tile_t = lambda bi, i: (bi, 0, 0, i)
    single = pl.Buffered(1)
    in_specs = [
        pl.BlockSpec((1, tok, D_MODEL), tile),
        pl.BlockSpec((1, CONV_HALO, D_MODEL),
                     lambda bi, i: (bi, jnp.maximum(i * halo_blocks - 1, 0), 0)),
        pl.BlockSpec((1, CONV_HALO, D_MODEL),
                     lambda bi, i: (bi, jnp.minimum((i + 1) * halo_blocks, n_halo - 1), 0)),
        pl.BlockSpec((None, 1, D_MODEL), lyr3),
        pl.BlockSpec((None, D_MODEL, IN_COLS), lyr3, pipeline_mode=single),
        pl.BlockSpec((tok, 2 * HEAD_W), lambda bi, i: (i, 0)),
        pl.BlockSpec((None, 1, SGU_W), lyr3),
        pl.BlockSpec((None, SGU_GROUPS, CHUNK, CHUNK), lambda bi, i: (layer, 0, 0, 0)),
        pl.BlockSpec((None, CHUNK, SGU_W), lyr3),
        pl.BlockSpec((None, SGU_W, D_MODEL), lyr3),
        pl.BlockSpec((None, CONV_K, SUBLANES, CONV_W), lambda bi, i: (layer, 0, 0, 0)),
        pl.BlockSpec((None, 1, CONV_W), lyr3),
        pl.BlockSpec((None, 1, CONV_W), lyr3),
        pl.BlockSpec((None, CONV_W, D_MODEL), lyr3),
    ]
    out_shape = [
        jax.ShapeDtypeStruct((b, N_HEADS, HEAD_W, seq), BF16),
        jax.ShapeDtypeStruct((b, N_HEADS, seq, HEAD_W), BF16),
        jax.ShapeDtypeStruct((b, N_HEADS, seq // KEY_BLOCK, HEAD_W, KEY_BLOCK), BF16),
        jax.ShapeDtypeStruct((b, seq, ATTN_W), BF16),
        jax.ShapeDtypeStruct((b, seq, D_MODEL), BF16),
        jax.ShapeDtypeStruct((b, seq, D_MODEL), BF16),
    ]
    out_specs = [
        pl.BlockSpec((1, N_HEADS, HEAD_W, tok), head_tile_t),
        pl.BlockSpec((1, N_HEADS, tok, HEAD_W), head_tile),
        pl.BlockSpec((1, N_HEADS, tok // KEY_BLOCK, HEAD_W, KEY_BLOCK), lambda bi, i: (bi, 0, i, 0, 0)),
        pl.BlockSpec((1, tok, ATTN_W), tile),
        pl.BlockSpec((1, tok, D_MODEL), tile),
        pl.BlockSpec((1, tok, D_MODEL), tile),
    ]
    return pl.pallas_call(
        functools.partial(_pre_kernel, rows=32),
        grid=grid,
        in_specs=in_specs,
        out_specs=out_specs,
        out_shape=out_shape,
        scratch_shapes=[pltpu.VMEM((tok, SGU_W), F32),
                        pltpu.VMEM((tok + 2 * CONV_HALO, CONV_W), F32),
                        pltpu.VMEM((tok, CONV_W), F32)],
        compiler_params=pltpu.CompilerParams(
            dimension_semantics=("arbitrary", "arbitrary"),
            vmem_limit_bytes=VMEM_LIMIT_BYTES),
        name="pre",
    )(x, x, x, g, w_in, cs, sgug, sguw, sgub, wc, cw, cb, cg, wa)


def _attn_kernel(lam_ref, g_ref, qt_ref, k_ref, vt_ref, o_ref,
                 q2_ref, mref_ref, kmax_ref, p0_ref, p1_ref, l_ref, *maybe_acc_ref,
                 lam_init, tqi, tk, body_steps):
    heads, tq = qt_ref.shape[1], qt_ref.shape[3]
    seq = k_ref.shape[2]
    nq = tq // tqi
    n = seq // tk
    nb = heads * nq
    items = nb * n
    r = 2 * tqi
    assert n & (n - 1) == 0 and nq & (nq - 1) == 0 and items >= 2
    log2_n = n.bit_length() - 1
    log2_nq = nq.bit_length() - 1
    lam = (jnp.exp(jnp.sum(lam_ref[0:1, :] * lam_ref[1:2, :], axis=-1, keepdims=True))
           - jnp.exp(jnp.sum(lam_ref[2:3, :] * lam_ref[3:4, :], axis=-1, keepdims=True))
           + lam_init)

    feat = lax.broadcasted_iota(jnp.int32, (HEAD_W, 1), 0)
    is_map0 = feat < HEAD_DIM

    @pl.when(pl.program_id(2) == 0)
    def _():
        lane_map0 = lax.broadcasted_iota(jnp.int32, (1, HEAD_W), 1) < HEAD_DIM
        sel = jnp.where(is_map0 == lane_map0, 1.0, 0.0).astype(BF16)
        col_map0 = lax.broadcasted_iota(jnp.int32, (1, r), 1) < tqi

        for hd in range(heads):
            def chunk_max(kc, best):
                k0 = pl.multiple_of(kc * tk, tk)
                kf = k_ref[0, hd, pl.ds(k0, tk), :].astype(F32)
                return jnp.maximum(best, jnp.max(_dot((kf * kf).astype(BF16), sel),
                                                 axis=0, keepdims=True))

            best = lax.fori_loop(0, n, chunk_max, jnp.zeros((1, HEAD_W), F32))
            best0 = jnp.max(jnp.where(lane_map0, best, 0.0), axis=1, keepdims=True)
            best1 = jnp.max(jnp.where(lane_map0, 0.0, best), axis=1, keepdims=True)
            kmax_ref[hd] = jnp.where(col_map0, best0, best1)

    zero = jnp.zeros((), BF16)
    for qb in range(nb):
        hd, j = divmod(qb, nq)
        qt = qt_ref[0, hd, :, j * tqi:(j + 1) * tqi]
        q2 = jnp.concatenate([jnp.where(is_map0, qt, zero), jnp.where(is_map0, zero, qt)], axis=1)
        q2_ref[qb] = q2
        q2f = q2.astype(F32)
        qn2 = jnp.sum(q2f * q2f, axis=0, keepdims=True)
        mref_ref[qb] = jnp.sqrt(qn2 * kmax_ref[hd]) * BOUND_SLACK

    def split(t):
        return lax.shift_right_logical(t, log2_n), lax.bitwise_and(t, n - 1)

    def k_chunk(qb, kc):
        hd = lax.shift_right_logical(qb, log2_nq)
        return k_ref[0, hd, pl.ds(pl.multiple_of(kc * tk, tk), tk), :]

    def vt_chunk(qb, kc):
        hd = lax.shift_right_logical(qb, log2_nq)
        kb0 = pl.multiple_of(kc * (tk // KEY_BLOCK), tk // KEY_BLOCK)
        return jnp.concatenate([vt_ref[0, hd, kb0 + i] for i in range(tk // KEY_BLOCK)], axis=1)

    def stage_qs(t, l, p_ref):
        qi, kc = split(t)
        p = jnp.exp2(_dot(k_chunk(qi, kc), q2_ref[qi]) - mref_ref[qi])
        p_ref[...] = p.astype(BF16)
        l = jnp.where(kc == 0, 0.0, l) + jnp.sum(p, axis=0, keepdims=True)
        return l

    gain = g_ref[...] * (1.0 - lam_init)

    def store_output(qb, acc, l):
        o = acc[:, :tqi] / l[:, :tqi] - lam * (acc[:, tqi:] / l[:, tqi:])
        o = o * lax.rsqrt(jnp.mean(o * o, axis=0, keepdims=True) + RMS_EPS) * gain
        if isinstance(qb, int):
            hd, q0 = qb // nq, (qb % nq) * tqi
        else:
            hd = lax.shift_right_logical(qb, log2_nq)
            q0 = pl.multiple_of(lax.bitwise_and(qb, nq - 1) * tqi, tqi)
        o_ref[0, hd, pl.ds(q0, tqi), :] = o.T.astype(o_ref.dtype)

    def stage_p(t, l, acc, p_ref):
        qi, kc = split(t)
        acc = jnp.where(kc == 0, 0.0, acc) + _dot(vt_chunk(qi, kc), p_ref[...])
        l_ref[qi] = l
        if maybe_acc_ref:
            maybe_acc_ref[0][qi] = acc
        else:
            store_output(qi, acc, l)
        return acc

    p_refs = (p0_ref, p1_ref)
    l0 = stage_qs(0, jnp.zeros((1, r), F32), p0_ref)

    def step(tau, c, carry):
        l_prev, acc = carry
        acc = stage_p(tau, l_prev, acc, p_refs[c])
        return stage_qs(tau + 1, l_prev, p_refs[1 - c]), acc

    def body(u, carry):
        for j in range(body_steps):
            carry = step(u * body_steps + j, j % 2, carry)
        return carry

    assert body_steps % 2 == 0
    trips = (items - 1) // body_steps
    carry = lax.fori_loop(0, trips, body, (l0, jnp.zeros((HEAD_W, r), F32)))
    for tau in range(trips * body_steps, items - 1):
        carry = step(tau, tau % 2, carry)
    stage_p(items - 1, *carry, p_refs[(items - 1) % 2])

    if maybe_acc_ref:
        for qb in range(nb):
            store_output(qb, maybe_acc_ref[0][qb], l_ref[qb])

    @pl.when(jnp.logical_not(jnp.min(l_ref[...]) >= MIN_DENOMINATOR))
    def _():
        def exact_block(j, carry):
            def chunk(kc, mla):
                m, l, acc = mla
                s = _dot(k_chunk(j, kc), q2_ref[j])
                m_new = jnp.maximum(m, jnp.max(s, axis=0, keepdims=True))
                alpha = jnp.exp2(m - m_new)
                p = jnp.exp2(s - m_new)
                l = alpha * l + jnp.sum(p, axis=0, keepdims=True)
                acc = alpha * acc + _dot(vt_chunk(j, kc), p.astype(BF16))
                return m_new, l, acc

            init = (jnp.full((1, r), -jnp.inf, F32), jnp.zeros((1, r), F32),
                    jnp.zeros((HEAD_W, r), F32))
            _, l, acc = lax.fori_loop(0, n, chunk, init)
            store_output(j, acc, l)
            return carry

        lax.fori_loop(0, nb, exact_block, 0)


def _attn_call(layer, lam_vecs, subln_g, qt, k, vt, lam_init):
    b, _, seq, _ = k.shape
    _, tq, tqi, tk, heads, body_steps = _tiles(seq)
    nb = heads * (tq // tqi)
    r = 2 * tqi
    grid = (b, N_HEADS // heads, seq // tq)
    lyr3 = lambda bi, hd, i: (layer, 0, 0)
    return pl.pallas_call(
        functools.partial(_attn_kernel, lam_init=lam_init, tqi=tqi, tk=tk, body_steps=body_steps),
        grid=grid,
        in_specs=[
            pl.BlockSpec((None, 4, HEAD_DIM), lyr3),
            pl.BlockSpec((None, HEAD_W, 1), lyr3),
            pl.BlockSpec((1, heads, HEAD_W, tq), lambda bi, hd, i: (bi, hd, 0, i)),
            pl.BlockSpec((1, heads, seq, HEAD_W), lambda bi, hd, i: (bi, hd, 0, 0)),
            pl.BlockSpec((1, heads, seq // KEY_BLOCK, HEAD_W, KEY_BLOCK),
                         lambda bi, hd, i: (bi, hd, 0, 0, 0)),
        ],
        out_specs=pl.BlockSpec((1, heads, tq, HEAD_W), lambda bi, hd, i: (bi, hd, i, 0)),
        out_shape=jax.ShapeDtypeStruct((b, N_HEADS, seq, HEAD_W), BF16),
        scratch_shapes=[
            pltpu.VMEM((nb, HEAD_W, r), BF16),
            pltpu.VMEM((nb, 1, r), F32),
            pltpu.VMEM((heads, 1, r), F32),
            pltpu.VMEM((tk, r), BF16), pltpu.VMEM((tk, r), BF16),
            pltpu.VMEM((nb, 1, r), F32),
            *([pltpu.VMEM((nb, HEAD_W, r), F32)] if seq > tk else []),
        ],
        compiler_params=pltpu.CompilerParams(
            dimension_semantics=("arbitrary", "arbitrary", "arbitrary"),
            vmem_limit_bytes=VMEM_LIMIT_BYTES),
        name="attn",
    )(lam_vecs, subln_g, qt, k, vt)


def _post_kernel(x_ref, o_ref, gb_ref, pc_ref, m1_ref, wb_ref, wo_ref, fg_ref, y_ref, *, final):
    o = jnp.concatenate([o_ref[0, hd] for hd in range(N_HEADS)], axis=1)
    yb = _dot((o.astype(F32) * gb_ref[0].astype(F32)).astype(BF16), wb_ref[...])
    y = m1_ref[0].astype(F32) * yb + pc_ref[0].astype(F32)
    out = x_ref[0] + _dot(y.astype(BF16), wo_ref[...])
    if final:
        out = _rms(out, fg_ref[...])
    y_ref[0] = out


def _post_call(layer, x, o, gb, pc, m1, wb, wo, fg, final):
    b, seq, _ = x.shape
    tok = POST_TOKENS
    grid = (b, seq // tok)
    tile = lambda bi, i: (bi, i, 0)
    lyr3 = lambda bi, i: (layer, 0, 0)
    in_specs = [
        pl.BlockSpec((1, tok, D_MODEL), tile),
        pl.BlockSpec((1, N_HEADS, tok, HEAD_W), lambda bi, i: (bi, 0, i, 0)),
        pl.BlockSpec((1, tok, ATTN_W), tile),
        pl.BlockSpec((1, tok, D_MODEL), tile),
        pl.BlockSpec((1, tok, D_MODEL), tile),
        pl.BlockSpec((None, ATTN_W, D_MODEL), lyr3),
        pl.BlockSpec((None, D_MODEL, D_MODEL), lyr3),
        pl.BlockSpec((1, D_MODEL), lambda bi, i: (0, 0)),
    ]
    return pl.pallas_call(
        functools.partial(_post_kernel, final=final),
        grid=grid,
        in_specs=in_specs,
        out_specs=pl.BlockSpec((1, tok, D_MODEL), tile),
        out_shape=jax.ShapeDtypeStruct((b, seq, D_MODEL), F32),
        compiler_params=pltpu.CompilerParams(
            dimension_semantics=("arbitrary", "arbitrary"),
            vmem_limit_bytes=VMEM_LIMIT_BYTES),
        name="post",
    )(x, o, gb, pc, m1, wb, wo, fg)


def _rotary_table(seq):
    half = HEAD_DIM // 2
    inv = ROPE_THETA ** (-jnp.arange(half, dtype=F32) / half)
    ang = jnp.arange(seq, dtype=jnp.int32).astype(F32)[:, None] * inv[None, :]
    cos = jnp.tile(jnp.cos(ang), (1, 4))
    sin = jnp.tile(jnp.sin(ang), (1, 4))
    sign = jnp.where(jnp.arange(HEAD_W) % HEAD_DIM < half, -1.0, 1.0).astype(F32)
    return jnp.concatenate([cos, sin * sign[None, :]], axis=1)


def kernel(x_prompt, x_sample, norm_g, w_in, conv_w, conv_b, conv_norm_g, w_proj_a,
           lam_q1, lam_k1, lam_q2, lam_k2, subln_g, w_proj_b,
           sgu_norm_g, sgu_w, sgu_b, w_proj_c, w_out, final_g):
    w_in_b = w_in.astype(BF16)
    sgu_w_b = sgu_w.astype(BF16)
    sgu_bias = jnp.repeat(jnp.swapaxes(sgu_b, 1, 2), SGU_GROUP_W, axis=2)
    wc_b = w_proj_c.astype(BF16)
    conv_w_rep = jnp.broadcast_to(conv_w[:, :, None, :], (DEPTH, CONV_K, SUBLANES, CONV_W))
    wa_b = w_proj_a.astype(BF16)
    wb_b = w_proj_b.astype(BF16)
    wo_b = w_out.astype(BF16)
    lam_vecs = jnp.stack([lam_q1, lam_k1, lam_q2, lam_k2], axis=1).astype(F32)
    cs = _rotary_table(max(x_prompt.shape[1], x_sample.shape[1]))

    row = lambda p: p[:, None, :]
    outs = []
    for x in (x_prompt, x_sample):
        for l in range(DEPTH):
            qt, k, vt, gb, pc, m1 = _pre_call(
                l, x, row(norm_g), w_in_b, cs, row(sgu_norm_g), sgu_w_b, sgu_bias, wc_b,
                conv_w_rep, row(conv_b), row(conv_norm_g), wa_b)
            o = _attn_call(l, lam_vecs, subln_g[:, :, None], qt, k, vt, _lambda_init(l))
            x = _post_call(l, x, o, gb, pc, m1, wb_b, wo_b, final_g[None],
                           final=(l == DEPTH - 1))
        outs.append(x)
    return tuple(outs)
```
